```python
import math
import jax
import jax.numpy as jnp
from jax import lax
import numpy as np

D_MODEL = 4096
BATCH = 1
SEQ = 16384
DEPTH = 4

GRID_W = 64
CTX_LEN = 256
BLOCK = 128
WINDOW = 128
ROPE_THETA = 10000.0
NORM_EPS = 1e-6
HEAD_DIM = 128

DA_HEADS = 8
DA_SUB = HEAD_DIM // 2
GQA_HEADS = 8
GQA_KV = 2
SWA_HEADS = 8
SWA_KV = 2
RW_HEADS = 16
RW_N = 64
RW_W = RW_HEADS * RW_N
RW_DECAY_R = 128
RW_A_R = 128
RW_GATE_R = 256
RW_LN_EPS = 64e-5
RW_DECAY_SCALE = 0.6065306597126334

N_BRANCH = 4
BRANCH_W = 1024
GATE_R = 256
MOD_R = 512
N_MOD = 6
D_FF = 8192
CONV_W = 3

DA_COLS = 3 * DA_HEADS * HEAD_DIM
GQA_COLS = (GQA_HEADS + 2 * GQA_KV) * HEAD_DIM
RW_COLS = 3 * RW_W + 2 * RW_DECAY_R + 2 * RW_A_R + RW_GATE_R
SWA_COLS = (SWA_HEADS + 2 * SWA_KV) * HEAD_DIM
IN_COLS = DA_COLS + GQA_COLS + RW_COLS + SWA_COLS + GATE_R
IN_SIZES = (DA_COLS, GQA_COLS, RW_COLS, SWA_COLS, GATE_R)
RW_SIZES = (RW_W, RW_W, RW_W, RW_DECAY_R, RW_DECAY_R, RW_A_R, RW_A_R, RW_GATE_R)

kernel_name = 'hybrid_parallel_mixer_dit_trunk'


def _cuts(sizes):
    out, acc = [], 0
    for s in sizes[:-1]:
        acc += s
        out.append(acc)
    return out


def rms_norm(x, gain, eps=NORM_EPS):
    xf = x.astype(jnp.float32)
    y = xf * lax.rsqrt(jnp.mean(xf * xf, axis=-1, keepdims=True) + eps)
    return (y * gain.astype(jnp.float32)).astype(x.dtype)


def modulate(x, shift, scale):
    return x * (1.0 + scale[:, None, :]) + shift[:, None, :]


def ada_modulation(cvec, w_down, w_up, bias):
    m = (jax.nn.silu(cvec) @ w_down) @ w_up + bias
    return m.reshape(cvec.shape[0], N_MOD, -1)


def axial_rope_tables(n_rows, dim):
    row = jnp.repeat(jnp.arange(n_rows, dtype=jnp.float32), GRID_W)
    col = jnp.tile(jnp.arange(GRID_W, dtype=jnp.float32), n_rows)
    quarter = dim // 4
    inv_freq = ROPE_THETA ** (-jnp.arange(quarter, dtype=jnp.float32) / quarter)
    ang_r = row[:, None] * inv_freq[None, :]
    ang_c = col[:, None] * inv_freq[None, :]
    ang = jnp.concatenate([ang_r, ang_r, ang_c, ang_c], axis=-1)
    return jnp.cos(ang), jnp.sin(ang)


def apply_rope(x, cos, sin):
    a1, a2, b1, b2 = jnp.split(x, 4, axis=-1)
    rot = jnp.concatenate([-a2, a1, -b2, b1], axis=-1)
    return x * cos[:, None, :] + rot * sin[:, None, :]


def depthwise_conv(x, w):
    pad = w.shape[0] // 2
    return lax.conv_general_dilated(
        x, w[:, None, :].astype(x.dtype), window_strides=(1,), padding=((pad, pad),),
        dimension_numbers=('NWC', 'WIO', 'NWC'), feature_group_count=x.shape[-1])


def sweep_blocks(fn, q):
    b, s = q.shape[:2]
    qb = jnp.moveaxis(q.reshape((b, s // BLOCK, BLOCK) + q.shape[2:]), 1, 0)
    o = jnp.moveaxis(lax.map(fn, qb), 0, 1)
    return o.reshape((b, s) + o.shape[3:])


def diff_attend(q, k, v, lam):
    s = jnp.einsum('bqhme,bthme->bhmqt', q, k) * (q.shape[-1] ** -0.5)
    p = jax.nn.softmax(s, axis=-1)
    a = p[:, :, 0] - lam * p[:, :, 1]
    return jnp.einsum('bhqt,bthd->bqhd', a, v)


def diff_attention(z, zc, lam_vec, subln, lam_init, cos, sin, need_ctx):
    def heads(t):
        bt, n = t.shape[:2]
        q, k, v = jnp.split(t.astype(jnp.float32), 3, axis=-1)
        return (q.reshape(bt, n, DA_HEADS * 2, DA_SUB), k.reshape(bt, n, DA_HEADS * 2, DA_SUB),
                v.reshape(bt, n, DA_HEADS, HEAD_DIM))
    q, k, v = heads(z)
    qc, kc, vc = heads(zc)
    b, s = z.shape[:2]
    nc = zc.shape[1]
    q = apply_rope(q, cos, sin).reshape(b, s, DA_HEADS, 2, DA_SUB)
    k = apply_rope(k, cos, sin).reshape(b, s, DA_HEADS, 2, DA_SUB)
    qc = qc.reshape(b, nc, DA_HEADS, 2, DA_SUB)
    kc = kc.reshape(b, nc, DA_HEADS, 2, DA_SUB)
    lv = lam_vec.astype(jnp.float32)
    lam = jnp.exp(jnp.sum(lv[0] * lv[1])) - jnp.exp(jnp.sum(lv[2] * lv[3])) + lam_init

    def post(o):
        return (rms_norm(o, subln) * (1.0 - lam_init)).reshape(o.shape[0], o.shape[1], -1)
    keys = jnp.concatenate([k, kc], axis=1)
    vals = jnp.concatenate([v, vc], axis=1)
    y = post(sweep_blocks(lambda qq: diff_attend(qq, keys, vals, lam), q))
    yc = post(diff_attend(qc, kc, vc, lam)) if need_ctx else None
    return y, yc


def gqa_attend(q, k, v):
    s = jnp.einsum('bqkgd,btkd->bkgqt', q, k) * (q.shape[-1] ** -0.5)
    p = jax.nn.softmax(s, axis=-1)
    return jnp.einsum('bkgqt,btkd->bqkgd', p, v)


def gqa_attention(z, zc, q_gain, k_gain, cos, sin, need_ctx):
    g = GQA_HEADS // GQA_KV

    def heads(t):
        bt, n = t.shape[:2]
        q, k, v = jnp.split(t.astype(jnp.float32),
                            [GQA_HEADS * HEAD_DIM, (GQA_HEADS + GQA_KV) * HEAD_DIM], axis=-1)
        q = rms_norm(q.reshape(bt, n, GQA_HEADS, HEAD_DIM), q_gain)
        k = rms_norm(k.reshape(bt, n, GQA_KV, HEAD_DIM), k_gain)
        return q, k, v.reshape(bt, n, GQA_KV, HEAD_DIM)
    q, k, v = heads(z)
    qc, kc, vc = heads(zc)
    b, s = z.shape[:2]
    nc = zc.shape[1]
    q = apply_rope(q, cos, sin).reshape(b, s, GQA_KV, g, HEAD_DIM)
    k = apply_rope(k, cos, sin)
    keys = jnp.concatenate([k, kc], axis=1)
    vals = jnp.concatenate([v, vc], axis=1)
    y = sweep_blocks(lambda qq: gqa_attend(qq, keys, vals), q).reshape(b, s, -1)
    yc = None
    if need_ctx:
        yc = gqa_attend(qc.reshape(b, nc, GQA_KV, g, HEAD_DIM), kc, vc).reshape(b, nc, -1)
    return y, yc


def rwkv_scan(r, w, k, v, a, bb, s0, reverse):
    def step(st, inp):
        rt, wt, kt, vt, at, bt = inp
        st = (st * wt[:, :, None, :]
              + jnp.einsum('bhij,bhj->bhi', st, at)[..., None] * bt[:, :, None, :]
              + vt[..., None] * kt[:, :, None, :])
        return st, jnp.einsum('bhij,bhj->bhi', st, rt)
    xs = tuple(jnp.moveaxis(t, 1, 0) for t in (r, w, k, v, a, bb))
    s_end, ys = lax.scan(step, s0, xs, reverse=reverse)
    return jnp.moveaxis(ys, 0, 1), s_end


def rwkv_prepare(t, shift_w, w0, w_up, a0, a_up, g_up, k_k, k_a):
    bt, n = t.shape[:2]
    t = depthwise_conv(t.astype(jnp.float32), shift_w.astype(jnp.float32))
    r, k, v, xw_f, xw_b, xa_f, xa_b, xg = jnp.split(t, _cuts(RW_SIZES), axis=-1)

    def heads(u):
        return u.reshape(bt, n, RW_HEADS, RW_N)
    kk = heads(k * k_k)
    kk = kk * lax.rsqrt(jnp.maximum(jnp.sum(kk * kk, axis=-1, keepdims=True), 1e-24))
    per_dir = []
    for d, (xw, xa) in enumerate(((xw_f, xa_f), (xw_b, xa_b))):
        decay = jnp.exp(-RW_DECAY_SCALE * jax.nn.sigmoid(w0[d] + jnp.tanh(xw) @ w_up[d]))
        icl = jax.nn.sigmoid(a0[d] + xa @ a_up[d])
        k_d = k * (1.0 + (icl - 1.0) * k_a)
        per_dir.append((heads(decay), heads(k_d), heads(icl)))
    gate = jax.nn.sigmoid(xg) @ g_up
    return heads(r), heads(v), kk, per_dir, gate


def rwkv_output(y, bonus, gate, ln_w, ln_b):
    bt, n = y.shape[:2]
    mu = jnp.mean(y, axis=-1, keepdims=True)
    var = jnp.mean(jnp.square(y - mu), axis=-1, keepdims=True)
    yn = ((y - mu) * lax.rsqrt(var + RW_LN_EPS)).reshape(bt, n, RW_W)
    yn = yn * ln_w.astype(jnp.float32) + ln_b.astype(jnp.float32)
    return (yn + bonus.reshape(bt, n, RW_W)) * gate


def rwkv_mixer(z, zc, shift_w, w0, w_up, a0, a_up, g_up, k_k, k_a, r_k, ln_w, ln_b, need_ctx):
    r, v, kk, dirs, gate = rwkv_prepare(z, shift_w, w0, w_up, a0, a_up, g_up, k_k, k_a)
    rc, vc, kkc, dirs_c, gate_c = rwkv_prepare(zc, shift_w, w0, w_up, a0, a_up, g_up, k_k, k_a)
    s0 = jnp.zeros((z.shape[0], RW_HEADS, RW_N, RW_N), jnp.float32)
    rk = r_k.astype(jnp.float32)
    y_sum, bonus, yc_sum, bonus_c = None, None, None, None
    for d, reverse in enumerate((False, True)):
        dec_c, k_c, icl_c = dirs_c[d]
        yc_d, s_ctx = rwkv_scan(rc, dec_c, k_c, vc, -kkc, kkc * icl_c, s0, reverse)
        dec, k_d, icl = dirs[d]
        y_d, _ = rwkv_scan(r, dec, k_d, v, -kk, kk * icl, s_ctx, reverse)
        b_d = jnp.sum(r * k_d * rk, axis=-1, keepdims=True) * v
        y_sum = y_d if y_sum is None else y_sum + y_d
        bonus = b_d if bonus is None else bonus + b_d
        if need_ctx:
            bc_d = jnp.sum(rc * k_c * rk, axis=-1, keepdims=True) * vc
            yc_sum = yc_d if yc_sum is None else yc_sum + yc_d
            bonus_c = bc_d if bonus_c is None else bonus_c + bc_d
    y = rwkv_output(y_sum, bonus, gate, ln_w, ln_b)
    yc = rwkv_output(yc_sum, bonus_c, gate_c, ln_w, ln_b) if need_ctx else None
    return y, yc


def swa_attention(z, zc, sink, cos, sin, need_ctx):
    g = SWA_HEADS // SWA_KV

    def heads(t):
        bt, n = t.shape[:2]
        q, k, v = jnp.split(t.astype(jnp.float32),
                            [SWA_HEADS * HEAD_DIM, (SWA_HEADS + SWA_KV) * HEAD_DIM], axis=-1)
        return (q.reshape(bt, n, SWA_HEADS, HEAD_DIM), k.reshape(bt, n, SWA_KV, HEAD_DIM),
                v.reshape(bt, n, SWA_KV, HEAD_DIM))
    q, k, v = heads(z)
    qc, kc, vc = heads(zc)
    b, s = z.shape[:2]
    nc = zc.shape[1]
    nb = s // BLOCK
    scale = HEAD_DIM ** -0.5
    sink_f = sink.astype(jnp.float32).reshape(SWA_KV, g)
    q = apply_rope(q, cos, sin).reshape(b, s, SWA_KV, g, HEAD_DIM)
    k = apply_rope(k, cos, sin)
    padw = ((0, 0), (BLOCK, BLOCK), (0, 0), (0, 0))
    kp = jnp.pad(k, padw)
    vp = jnp.pad(v, padw)
    qb = jnp.moveaxis(q.reshape(b, nb, BLOCK, SWA_KV, g, HEAD_DIM), 1, 0)

    def one(args):
        n, qq = args
        start = n * BLOCK
        kw = lax.dynamic_slice_in_dim(kp, start, 3 * BLOCK, axis=1)
        vw = lax.dynamic_slice_in_dim(vp, start, 3 * BLOCK, axis=1)
        qpos = start + jnp.arange(BLOCK)
        kpos = start - BLOCK + jnp.arange(3 * BLOCK)
        ok = ((jnp.abs(qpos[:, None] - kpos[None, :]) <= WINDOW)
              & (kpos >= 0)[None, :] & (kpos < s)[None, :])
        s_win = jnp.where(ok, jnp.einsum('bqkgd,btkd->bkgqt', qq, kw) * scale, -jnp.inf)
        s_ctx = jnp.einsum('bqkgd,btkd->bkgqt', qq, kc) * scale
        s_snk = jnp.broadcast_to(sink_f[None, :, :, None, None], s_ctx.shape[:-1] + (1,))
        p = jax.nn.softmax(jnp.concatenate([s_win, s_ctx, s_snk], axis=-1), axis=-1)
        return (jnp.einsum('bkgqt,btkd->bqkgd', p[..., :3 * BLOCK], vw)
                + jnp.einsum('bkgqt,btkd->bqkgd', p[..., 3 * BLOCK:-1], vc))
    o = jnp.moveaxis(lax.map(one, (jnp.arange(nb), qb)), 0, 1)
    y = o.reshape(b, s, -1)
    yc = None
    if need_ctx:
        qcg = qc.reshape(b, nc, SWA_KV, g, HEAD_DIM)
        s_c = jnp.einsum('bqkgd,btkd->bkgqt', qcg, kc) * scale
        s_snk = jnp.broadcast_to(sink_f[None, :, :, None, None], s_c.shape[:-1] + (1,))
        p = jax.nn.softmax(jnp.concatenate([s_c, s_snk], axis=-1), axis=-1)
        yc = jnp.einsum('bkgqt,btkd->bqkgd', p[..., :-1], vc).reshape(b, nc, -1)
    return y, yc


def merge_branches(ys, zg, gate_up, gate_bias, branch_up, w_out):
    zg = zg.astype(jnp.float32)
    acc = None
    for bi in range(N_BRANCH):
        term = jax.nn.sigmoid(zg @ gate_up[bi] + gate_bias[bi]) * (ys[bi] @ branch_up[bi])
        acc = term if acc is None else acc + term
    return acc @ w_out


def conv_ffn(u, w_up, w_conv, w_down):
    hid = depthwise_conv(u @ w_up, w_conv)
    gate, val = jnp.split(hid, 2, axis=-1)
    return (jax.nn.silu(gate) * val) @ w_down


def setup_inputs(seed: int = 0) -> dict:
    key = jax.random.key(seed)
    keys = jax.random.split(key, 40)
    counter = [0]

    def nrm(shape, scale):
        kk = keys[counter[0]]
        counter[0] += 1
        return jax.random.normal(kk, shape, jnp.float32) * scale

    def gain(shape):
        return 1.0 + nrm(shape, 0.02)
    L, D = DEPTH, D_MODEL
    return {
        'x': nrm((BATCH, SEQ, D), 1.0),
        'c': nrm((BATCH, D), 1.0),
        'ctx': nrm((BATCH, CTX_LEN, D), 1.0),
        'c_ctx': nrm((D,), 1.0),
        'mod_down': nrm((L, D, MOD_R), D ** -0.5),
        'mod_up': nrm((L, MOD_R, N_MOD * D), 0.5 * MOD_R ** -0.5),
        'mod_bias': nrm((L, N_MOD * D), 0.01),
        'norm_mix_pre': gain((L, D)),
        'norm_mix_post': gain((L, D)),
        'norm_ffn_pre': gain((L, D)),
        'norm_ffn_post': gain((L, D)),
        'w_in': nrm((L, D, IN_COLS), D ** -0.5),
        'diff_lambda': nrm((L, 4, DA_SUB), 0.1),
        'diff_subln': gain((L, HEAD_DIM)),
        'gqa_q_norm': gain((L, HEAD_DIM)),
        'gqa_k_norm': gain((L, HEAD_DIM)),
        'rwkv_shift': nrm((L, CONV_W, RW_COLS), 0.5),
        'rwkv_w0': nrm((L, 2, RW_W), 1.0),
        'rwkv_w_up': nrm((L, 2, RW_DECAY_R, RW_W), 0.5 * RW_DECAY_R ** -0.5),
        'rwkv_a0': nrm((L, 2, RW_W), 0.5),
        'rwkv_a_up': nrm((L, 2, RW_A_R, RW_W), 0.5 * RW_A_R ** -0.5),
        'rwkv_g_up': nrm((L, RW_GATE_R, RW_W), RW_GATE_R ** -0.5),
        'rwkv_k_k': gain((L, RW_W)),
        'rwkv_k_a': gain((L, RW_W)),
        'rwkv_r_k': nrm((L, RW_HEADS, RW_N), 0.1),
        'rwkv_ln_w': gain((L, RW_W)),
        'rwkv_ln_b': nrm((L, RW_W), 0.01),
        'swa_sink': nrm((L, SWA_HEADS), 1.0),
        'branch_up': nrm((L, N_BRANCH, BRANCH_W, D), BRANCH_W ** -0.5),
        'gate_up': nrm((L, N_BRANCH, GATE_R, D), GATE_R ** -0.5),
        'gate_bias': nrm((L, N_BRANCH, D), 0.1),
        'w_out': nrm((L, D, D), D ** -0.5),
        'ffn_up': nrm((L, D, 2 * D_FF), D ** -0.5),
        'ffn_conv': nrm((L, CONV_W, 2 * D_FF), 0.5),
        'ffn_down': nrm((L, D_FF, D), D_FF ** -0.5),
    }


def reference(x, c, ctx, c_ctx, mod_down, mod_up, mod_bias, norm_mix_pre, norm_mix_post,
              norm_ffn_pre, norm_ffn_post, w_in, diff_lambda, diff_subln, gqa_q_norm, gqa_k_norm,
              rwkv_shift, rwkv_w0, rwkv_w_up, rwkv_a0, rwkv_a_up, rwkv_g_up, rwkv_k_k, rwkv_k_a,
              rwkv_r_k, rwkv_ln_w, rwkv_ln_b, swa_sink, branch_up, gate_up, gate_bias, w_out,
              ffn_up, ffn_conv, ffn_down):
    s = x.shape[1]
    n_rows = s // GRID_W
    cos_h, sin_h = axial_rope_tables(n_rows, HEAD_DIM)
    cos_s, sin_s = axial_rope_tables(n_rows, DA_SUB)
    cuts = _cuts(IN_SIZES)
    h, hc = x, ctx
    for l in range(DEPTH):
        need_ctx = l < DEPTH - 1
        lam_init = 0.8 - 0.6 * math.exp(-0.3 * l)
        m = ada_modulation(c, mod_down[l], mod_up[l], mod_bias[l])
        mc = ada_modulation(c_ctx[None, :], mod_down[l], mod_up[l], mod_bias[l])
        u = modulate(rms_norm(h, norm_mix_pre[l]), m[:, 0], m[:, 1])
        uc = modulate(rms_norm(hc, norm_mix_pre[l]), mc[:, 0], mc[:, 1])
        za, zb, zr, zd, zg = jnp.split(u @ w_in[l], cuts, axis=-1)
        zac, zbc, zrc, zdc, zgc = jnp.split(uc @ w_in[l], cuts, axis=-1)
        ya, yac = diff_attention(za, zac, diff_lambda[l], diff_subln[l], lam_init, cos_s, sin_s, need_ctx)
        yb, ybc = gqa_attention(zb, zbc, gqa_q_norm[l], gqa_k_norm[l], cos_h, sin_h, need_ctx)
        yr, yrc = rwkv_mixer(zr, zrc, rwkv_shift[l], rwkv_w0[l], rwkv_w_up[l], rwkv_a0[l], rwkv_a_up[l],
                             rwkv_g_up[l], rwkv_k_k[l], rwkv_k_a[l], rwkv_r_k[l], rwkv_ln_w[l],
                             rwkv_ln_b[l], need_ctx)
        yd, ydc = swa_attention(zd, zdc, swa_sink[l], cos_h, sin_h, need_ctx)
        mix = merge_branches((ya, yb, yr, yd), zg, gate_up[l], gate_bias[l], branch_up[l], w_out[l])
        h = (h + m[:, 2][:, None, :] * rms_norm(mix, norm_mix_post[l])).astype(x.dtype)
        f = conv_ffn(modulate(rms_norm(h, norm_ffn_pre[l]), m[:, 3], m[:, 4]),
                     ffn_up[l], ffn_conv[l], ffn_down[l])
        h = (h + m[:, 5][:, None, :] * rms_norm(f, norm_ffn_post[l])).astype(x.dtype)
        if need_ctx:
            mixc = merge_branches((yac, ybc, yrc, ydc), zgc, gate_up[l], gate_bias[l], branch_up[l], w_out[l])
            hc = (hc + mc[:, 2][:, None, :] * rms_norm(mixc, norm_mix_post[l])).astype(ctx.dtype)
            fc = conv_ffn(modulate(rms_norm(hc, norm_ffn_pre[l]), mc[:, 3], mc[:, 4]),
                          ffn_up[l], ffn_conv[l], ffn_down[l])
            hc = (hc + mc[:, 5][:, None, :] * rms_norm(fc, norm_ffn_post[l])).astype(ctx.dtype)
    return h
```

```python
import functools
import math

import jax
import jax.numpy as jnp
from jax import lax
from jax.experimental import pallas as pl
from jax.experimental.pallas import tpu as pltpu

F32 = jnp.float32
BF16 = jnp.bfloat16
HI = lax.Precision.HIGHEST

GRID_W = 64
BLOCK = 128
WINDOW = 128
ROPE_THETA = 10000.0
NORM_EPS = 1e-6
HEAD_DIM = 128
DA_HEADS = 8
DA_SUB = HEAD_DIM // 2
GQA_HEADS = 8
GQA_KV = 2
SWA_HEADS = 8
SWA_KV = 2
RW_HEADS = 16
RW_N = 64
RW_W = RW_HEADS * RW_N
RW_DECAY_R = 128
RW_A_R = 128
RW_GATE_R = 256
RW_LN_EPS = 64e-5
RW_DECAY_SCALE = 0.6065306597126334
N_BRANCH = 4
BRANCH_W = 1024
GATE_R = 256
N_MOD = 6
DA_COLS = 3 * DA_HEADS * HEAD_DIM
GQA_COLS = (GQA_HEADS + 2 * GQA_KV) * HEAD_DIM
RW_COLS = 3 * RW_W + 2 * RW_DECAY_R + 2 * RW_A_R + RW_GATE_R
SWA_COLS = (SWA_HEADS + 2 * SWA_KV) * HEAD_DIM
IN_SIZES = (DA_COLS, GQA_COLS, RW_COLS, SWA_COLS, GATE_R)

LANES = 128
SUBLANES = 8
VMEM_LIMIT = 56 * 1024 * 1024
RW_CHUNK = 64
_RW_N_BITS = RW_N.bit_length() - 1
assert RW_CHUNK == RW_N == 1 << _RW_N_BITS
NEG_BIG = -1e30

_NT = (((1,), (1,)), ((), ()))
_TN = (((0,), (0,)), ((), ()))


def _params(*sem):
    return pltpu.CompilerParams(dimension_semantics=sem, vmem_limit_bytes=VMEM_LIMIT)


def _row(v):
    return v.reshape(1, -1)


def _mm_kernel(*refs, nk, has_bias, pre_silu):
    a_ref, b_ref = refs[0], refs[1]
    bias_ref = refs[2] if has_bias else None
    o_ref = refs[3] if has_bias else refs[2]
    a = a_ref[...]
    if pre_silu:
        a = (a * jax.nn.sigmoid(a)).astype(BF16)
    prod = jnp.dot(a, b_ref[...], preferred_element_type=F32)

    def finish(acc):
        if has_bias:
            acc = acc + bias_ref[...]
        o_ref[...] = acc.astype(o_ref.dtype)

    if nk == 1:
        finish(prod)
        return
    acc_ref = refs[-1]
    k = pl.program_id(2)

    @pl.when(k == 0)
    def _():
        acc_ref[...] = prod

    @pl.when(k > 0)
    def _():
        acc_ref[...] += prod

    @pl.when(k == nk - 1)
    def _():
        finish(acc_ref[...])


def matmul(a, b, *, out_dtype=F32, tm=1024, tn=512, tk=None, bias=None, pre_silu=False):
    m, kd = a.shape
    n = b.shape[1]
    tm, tn = min(tm, m), min(tn, n)
    tk = kd if tk is None else min(tk, kd)
    assert m % tm == 0 and n % tn == 0 and kd % tk == 0, (a.shape, b.shape, tm, tn, tk)
    nk = kd // tk
    in_specs = [pl.BlockSpec((tm, tk), lambda i, j, k: (i, k)),
                pl.BlockSpec((tk, tn), lambda i, j, k: (k, j))]
    args = [a, b]
    if bias is not None:
        in_specs.append(pl.BlockSpec((1, tn), lambda i, j, k: (0, j)))
        args.append(bias)
    return pl.pallas_call(
        functools.partial(_mm_kernel, nk=nk, has_bias=bias is not None, pre_silu=pre_silu),
        grid=(m // tm, n // tn, nk),
        in_specs=in_specs,
        out_specs=pl.BlockSpec((tm, tn), lambda i, j, k: (i, j)),
        out_shape=jax.ShapeDtypeStruct((m, n), out_dtype),
        scratch_shapes=[pltpu.VMEM((tm, tn), F32)] if nk > 1 else [],
        compiler_params=_params("parallel", "parallel", "arbitrary"),
    )(*args)


def _rms(x, gain):
    return x * lax.rsqrt(jnp.mean(x * x, axis=-1, keepdims=True) + NORM_EPS) * gain


def _rms_mod_kernel(h_ref, g_ref, sh_ref, sc_ref, o_ref):
    y = _rms(h_ref[...], g_ref[...])
    o_ref[...] = (y * (1.0 + sc_ref[...]) + sh_ref[...]).astype(o_ref.dtype)


def rms_mod(h, gain, shift, scale, *, tm=256):
    m, d = h.shape
    tm = min(tm, m)
    vec = pl.BlockSpec((1, d), lambda i: (0, 0))
    return pl.pallas_call(
        _rms_mod_kernel, grid=(m // tm,),
        in_specs=[pl.BlockSpec((tm, d), lambda i: (i, 0)), vec, vec, vec],
        out_specs=pl.BlockSpec((tm, d), lambda i: (i, 0)),
        out_shape=jax.ShapeDtypeStruct((m, d), BF16),
        compiler_params=_params("parallel"),
    )(h, _row(gain), _row(shift), _row(scale))


def _resid_kernel(*refs, with_next):
    h_ref, y_ref, gp_ref, gate_ref = refs[:4]
    h_new = h_ref[...] + gate_ref[...] * _rms(y_ref[...], gp_ref[...])
    if with_next:
        gn_ref, sh_ref, sc_ref, o_ref, u_ref = refs[4:]
        u = _rms(h_new, gn_ref[...])
        u_ref[...] = (u * (1.0 + sc_ref[...]) + sh_ref[...]).astype(u_ref.dtype)
    else:
        o_ref = refs[4]
    o_ref[...] = h_new


def resid_norm(h, y, gain_post, gate, nxt=None, *, tm=256):
    m, d = h.shape
    tm = min(tm, m)
    vec = pl.BlockSpec((1, d), lambda i: (0, 0))
    blk = pl.BlockSpec((tm, d), lambda i: (i, 0))
    args = [h, y, _row(gain_post), _row(gate)]
    in_specs = [blk, blk, vec, vec]
    out_specs, out_shape = blk, jax.ShapeDtypeStruct((m, d), F32)
    if nxt is not None:
        args += [_row(v) for v in nxt]
        in_specs += [vec, vec, vec]
        out_specs = (blk, blk)
        out_shape = (out_shape, jax.ShapeDtypeStruct((m, d), BF16))
    return pl.pallas_call(
        functools.partial(_resid_kernel, with_next=nxt is not None), grid=(m // tm,),
        in_specs=in_specs, out_specs=out_specs, out_shape=out_shape,
        compiler_params=_params("parallel"),
    )(*args)


def rope_tables(n_rows, dim):
    row = jnp.repeat(jnp.arange(n_rows, dtype=F32), GRID_W)
    col = jnp.tile(jnp.arange(GRID_W, dtype=F32), n_rows)
    quarter = dim // 4
    inv_freq = ROPE_THETA ** (-jnp.arange(quarter, dtype=F32) / quarter)
    ang_r = row[:, None] * inv_freq[None, :]
    ang_c = col[:, None] * inv_freq[None, :]
    ang = jnp.concatenate([ang_r, ang_r, ang_c, ang_c], axis=-1)
    cos, sin = jnp.cos(ang), jnp.sin(ang)
    first = (jnp.arange(dim) % (2 * quarter)) < quarter
    sin_p = jnp.where(first[None, :], -sin, 0.0)
    sin_m = jnp.where(first[None, :], 0.0, sin)
    rep = LANES // dim
    return tuple(jnp.tile(t, (1, rep)) for t in (cos, sin_p, sin_m))


def _prep_kernel(*refs, n_groups, has_gain, has_rope, quarter, scale):
    refs = list(refs)
    z_ref = refs.pop(0)
    gain_ref = refs.pop(0) if has_gain else None
    if has_rope:
        cos = refs.pop(0)[...]
        sin_p = refs.pop(0)[...]
        sin_m = refs.pop(0)[...]
    o_ref = refs.pop(0)
    for j in range(n_groups):
        x = z_ref[:, j * LANES:(j + 1) * LANES]
        if has_gain:
            x = _rms(x, gain_ref[...])
        if has_rope:
            x = (x * cos + pltpu.roll(x, LANES - quarter, 1) * sin_p
                 + pltpu.roll(x, quarter, 1) * sin_m)
        if scale != 1.0:
            x = x * scale
        o_ref[:, j * LANES:(j + 1) * LANES] = x.astype(o_ref.dtype)


def prep(z, col_block, width, *, gain=None, rope=None, quarter=0, scale=1.0, tm=256):
    m = z.shape[0]
    tm = min(tm, m)
    args = [z]
    in_specs = [pl.BlockSpec((tm, width), lambda i: (i, col_block))]
    if gain is not None:
        args.append(_row(gain))
        in_specs.append(pl.BlockSpec((1, LANES), lambda i: (0, 0)))
    if rope is not None:
        args += list(rope)
        in_specs += [pl.BlockSpec((tm, LANES), lambda i: (i, 0))] * 3
    return pl.pallas_call(
        functools.partial(_prep_kernel, n_groups=width // LANES, has_gain=gain is not None,
                          has_rope=rope is not None, quarter=quarter, scale=scale),
        grid=(m // tm,), in_specs=in_specs,
        out_specs=pl.BlockSpec((tm, width), lambda i: (i, 0)),
        out_shape=jax.ShapeDtypeStruct((m, width), BF16),
        compiler_params=_params("parallel"),
    )(*args)


def _flash_kernel(*refs, mode, g, tq, tk, n_chunks, has_sink, lam_init):
    refs = list(refs)
    q_ref, k_ref, v_ref, kc_ref, vc_ref = refs[:5]
    rest = refs[5:]
    if mode == "diff":
        lam_ref, subln_ref, o_ref, m_scr, l_scr, acc_scr = rest
    elif has_sink:
        sink_ref, o_ref, m_scr, l_scr, acc_scr = rest
    else:
        o_ref, m_scr, l_scr, acc_scr = rest

    if mode == "diff":
        q = q_ref[...]
        lane = lax.broadcasted_iota(jnp.int32, q.shape, 1)
        zero = jnp.zeros_like(q)
        qs = jnp.concatenate([jnp.where(lane < DA_SUB, q, zero),
                              jnp.where(lane >= DA_SUB, q, zero)], axis=0)
    else:
        qs = jnp.concatenate([q_ref[:, h * LANES:(h + 1) * LANES] for h in range(g)], axis=0)

    sink_col = None
    if has_sink:
        kv = pl.program_id(0)
        sink_col = jnp.concatenate(
            [jnp.full((tq, 1), sink_ref[kv * g + h], F32) for h in range(g)], axis=0)

    s = lax.dot_general(qs, kc_ref[...], _NT, preferred_element_type=F32)
    m = jnp.max(s, axis=-1, keepdims=True)
    if has_sink:
        m = jnp.maximum(m, sink_col)
    p = jnp.exp(s - m)
    l = jnp.sum(p, axis=-1, keepdims=True)
    acc = jnp.dot(p.astype(BF16), vc_ref[...], preferred_element_type=F32)

    if n_chunks > 0:
        m_scr[...] = m
        l_scr[...] = l
        acc_scr[...] = acc

        def body(c, carry):
            off = pl.multiple_of(c * tk, tk)
            kb = k_ref[pl.ds(off, tk), :]
            vb = v_ref[pl.ds(off, tk), :]
            sb = lax.dot_general(qs, kb, _NT, preferred_element_type=F32)
            m_prev = m_scr[...]
            m_new = jnp.maximum(m_prev, jnp.max(sb, axis=-1, keepdims=True))
            alpha = jnp.exp(m_prev - m_new)
            pb = jnp.exp(sb - m_new)
            l_scr[...] = alpha * l_scr[...] + jnp.sum(pb, axis=-1, keepdims=True)
            acc_scr[...] = alpha * acc_scr[...] + jnp.dot(pb.astype(BF16), vb,
                                                          preferred_element_type=F32)
            m_scr[...] = m_new
            return carry

        lax.fori_loop(0, n_chunks, body, 0)
        m, l, acc = m_scr[...], l_scr[...], acc_scr[...]

    if has_sink:
        l = l + jnp.exp(sink_col - m)
    out = acc / l
    if mode == "diff":
        lv = lam_ref[...]
        lam = (jnp.exp(jnp.sum(lv[0:1] * lv[1:2], axis=-1, keepdims=True))
               - jnp.exp(jnp.sum(lv[2:3] * lv[3:4], axis=-1, keepdims=True)) + lam_init)
        y = out[:tq] - lam * out[tq:]
        o_ref[...] = (_rms(y, subln_ref[...]) * (1.0 - lam_init)).astype(o_ref.dtype)
    else:
        for h in range(g):
            o_ref[:, h * LANES:(h + 1) * LANES] = out[h * tq:(h + 1) * tq].astype(o_ref.dtype)


def flash(q, k, v, kc, vc, *, mode, tq, tk=512, sink=None, lam=None, subln=None, lam_init=0.0):
    sq = q.shape[0]
    n_ctx = kc.shape[0]
    if k is None:
        k, v, n_chunks, t_main = kc, vc, 0, n_ctx
    else:
        t_main = k.shape[0]
        tk = min(tk, t_main)
        n_chunks = t_main // tk
    tq = min(tq, sq)
    if mode == "diff":
        n_outer, g, qw = DA_HEADS, 2, LANES
    else:
        n_outer, g = k.shape[1] // LANES, q.shape[1] // k.shape[1]
        qw = g * LANES
    rows = g * tq
    kv_spec = pl.BlockSpec((t_main, LANES), lambda h, i: (0, h))
    ctx_spec = pl.BlockSpec((n_ctx, LANES), lambda h, i: (0, h))
    in_specs = [pl.BlockSpec((tq, qw), lambda h, i: (i, h)), kv_spec, kv_spec, ctx_spec, ctx_spec]
    args = [q, k, v, kc, vc]
    if mode == "diff":
        in_specs += [pl.BlockSpec(lam.shape, lambda h, i: (0, 0)),
                     pl.BlockSpec((1, LANES), lambda h, i: (0, 0))]
        args += [lam, _row(subln)]
    elif sink is not None:
        in_specs.append(pl.BlockSpec(memory_space=pltpu.SMEM))
        args.append(sink)
    return pl.pallas_call(
        functools.partial(_flash_kernel, mode=mode, g=g, tq=tq, tk=tk, n_chunks=n_chunks,
                          has_sink=sink is not None, lam_init=lam_init),
        grid=(n_outer, sq // tq), in_specs=in_specs,
        out_specs=pl.BlockSpec((tq, qw), lambda h, i: (i, h)),
        out_shape=jax.ShapeDtypeStruct((sq, n_outer * qw), BF16),
        scratch_shapes=[pltpu.VMEM((rows, 1), F32), pltpu.VMEM((rows, 1), F32),
                        pltpu.VMEM((rows, LANES), F32)],
        compiler_params=_params("parallel", "arbitrary"),
    )(*args)


def _swa_kernel(q_ref, kp_ref, kn_ref, kx_ref, vp_ref, vn_ref, vx_ref, kc_ref, vc_ref, sink_ref,
                o_ref, *, g, seq):
    kv = pl.program_id(0)
    n = pl.program_id(1)
    qs = jnp.concatenate([q_ref[:, h * LANES:(h + 1) * LANES] for h in range(g)], axis=0)
    kw = jnp.concatenate([kp_ref[...], kn_ref[...], kx_ref[...]], axis=0)
    vw = jnp.concatenate([vp_ref[...], vn_ref[...], vx_ref[...]], axis=0)
    s_win = lax.dot_general(qs, kw, _NT, preferred_element_type=F32)
    rows = g * BLOCK
    start = n * BLOCK
    qpos = start + (lax.broadcasted_iota(jnp.int32, (rows, 3 * BLOCK), 0) & (BLOCK - 1))
    kpos = start - BLOCK + lax.broadcasted_iota(jnp.int32, (rows, 3 * BLOCK), 1)
    ok = (jnp.abs(qpos - kpos) <= WINDOW) & (kpos >= 0) & (kpos < seq)
    s_win = jnp.where(ok, s_win, NEG_BIG)
    s_ctx = lax.dot_general(qs, kc_ref[...], _NT, preferred_element_type=F32)
    sink_col = jnp.concatenate(
        [jnp.full((BLOCK, 1), sink_ref[kv * g + h], F32) for h in range(g)], axis=0)
    m = jnp.maximum(jnp.maximum(jnp.max(s_win, axis=-1, keepdims=True),
                                jnp.max(s_ctx, axis=-1, keepdims=True)), sink_col)
    p_win = jnp.exp(s_win - m)
    p_ctx = jnp.exp(s_ctx - m)
    l = (jnp.sum(p_win, axis=-1, keepdims=True) + jnp.sum(p_ctx, axis=-1, keepdims=True)
         + jnp.exp(sink_col - m))
    out = (jnp.dot(p_win.astype(BF16), vw, preferred_element_type=F32)
           + jnp.dot(p_ctx.astype(BF16), vc_ref[...], preferred_element_type=F32)) / l
    for h in range(g):
        o_ref[:, h * LANES:(h + 1) * LANES] = out[h * BLOCK:(h + 1) * BLOCK].astype(o_ref.dtype)


def swa(q, k, v, kc, vc, sink):
    seq = q.shape[0]
    nb = seq // BLOCK
    n_kv = k.shape[1] // LANES
    g = q.shape[1] // k.shape[1]
    n_ctx = kc.shape[0]
    prev = pl.BlockSpec((BLOCK, LANES), lambda h, i: (jnp.maximum(i - 1, 0), h))
    cur = pl.BlockSpec((BLOCK, LANES), lambda h, i: (i, h))
    nxt = pl.BlockSpec((BLOCK, LANES), lambda h, i: (jnp.minimum(i + 1, nb - 1), h))
    ctx_spec = pl.BlockSpec((n_ctx, LANES), lambda h, i: (0, h))
    qspec = pl.BlockSpec((BLOCK, g * LANES), lambda h, i: (i, h))
    return pl.pallas_call(
        functools.partial(_swa_kernel, g=g, seq=seq), grid=(n_kv, nb),
        in_specs=[qspec, prev, cur, nxt, prev, cur, nxt, ctx_spec, ctx_spec,
                  pl.BlockSpec(memory_space=pltpu.SMEM)],
        out_specs=qspec,
        out_shape=jax.ShapeDtypeStruct(q.shape, BF16),
        compiler_params=_params("parallel", "parallel"),
    )(q, k, k, k, v, v, v, kc, vc, sink)


def _head_ones():
    r = lax.broadcasted_iota(jnp.int32, (LANES, LANES), 0) >> _RW_N_BITS
    c = lax.broadcasted_iota(jnp.int32, (LANES, LANES), 1) >> _RW_N_BITS
    return (r == c).astype(F32)


def _head_sum(x, ones):
    parts = [jnp.dot(x[:, j * LANES:(j + 1) * LANES], ones, precision=HI,
                     preferred_element_type=F32) for j in range(x.shape[1] // LANES)]
    return jnp.concatenate(parts, axis=1)


def _shifted(z, prev_row, next_row):
    tm = z.shape[0]
    rid = lax.broadcasted_iota(jnp.int32, z.shape, 0)
    zm = jnp.where(rid == 0, prev_row, pltpu.roll(z, 1, 0))
    zp = jnp.where(rid == tm - 1, next_row, pltpu.roll(z, tm - 1, 0))
    return zm, zp


def _halo_rows(zp_ref, zn_ref, n_blocks):
    i = pl.program_id(0)
    prev_row = jnp.where(i > 0, zp_ref[SUBLANES - 1:SUBLANES, :], 0.0)
    next_row = jnp.where(i < n_blocks - 1, zn_ref[0:1, :], 0.0)
    return prev_row, next_row


def _rw_prep_kernel(z_ref, zp_ref, zn_ref, sw_ref, kk_ref, w0_ref, wup_ref, a0_ref, aup_ref,
                    gup_ref, r_o, k_o, v_o, kk_o, lwf_o, lwb_o, iclf_o, iclb_o, gate_o, *, n_blocks):
    z = z_ref[...]
    prev_row, next_row = _halo_rows(zp_ref, zn_ref, n_blocks)
    zm, zp = _shifted(z, prev_row, next_row)
    t = zm * sw_ref[0:1, :] + z * sw_ref[1:2, :] + zp * sw_ref[2:3, :]
    r_o[...] = t[:, 0:RW_W]
    k = t[:, RW_W:2 * RW_W]
    k_o[...] = k
    v_o[...] = t[:, 2 * RW_W:3 * RW_W]
    kk = k * kk_ref[...]
    ss = _head_sum(kk * kk, _head_ones())
    kk_o[...] = kk * lax.rsqrt(jnp.maximum(ss, 1e-24))
    base = 3 * RW_W
    for d, (lw_o, icl_o) in enumerate(((lwf_o, iclf_o), (lwb_o, iclb_o))):
        xw = t[:, base + d * RW_DECAY_R: base + (d + 1) * RW_DECAY_R]
        pre = w0_ref[d:d + 1, :] + jnp.dot(jnp.tanh(xw).astype(BF16), wup_ref[d],
                                           preferred_element_type=F32)
        lw_o[...] = -RW_DECAY_SCALE * jax.nn.sigmoid(pre)
        a_base = base + 2 * RW_DECAY_R
        xa = t[:, a_base + d * RW_A_R: a_base + (d + 1) * RW_A_R]
        icl_o[...] = jax.nn.sigmoid(a0_ref[d:d + 1, :] + jnp.dot(
            xa.astype(BF16), aup_ref[d], preferred_element_type=F32))
    xg = t[:, base + 2 * RW_DECAY_R + 2 * RW_A_R:]
    gate_o[...] = jnp.dot(jax.nn.sigmoid(xg).astype(BF16), gup_ref[...],
                          preferred_element_type=F32)


def rw_prep(zr, shift_w, k_k, w0, w_up, a0, a_up, g_up, *, tm=128):
    m, w = zr.shape
    tm = min(tm, m)
    nb = m // tm
    per8 = tm // SUBLANES
    last8 = m // SUBLANES - 1
    full = lambda a: pl.BlockSpec(a.shape, lambda i: (0,) * a.ndim)
    out_blk = pl.BlockSpec((tm, RW_W), lambda i: (i, 0))
    args = [zr, zr, zr, shift_w, _row(k_k), w0, w_up, a0, a_up, g_up]
    in_specs = [pl.BlockSpec((tm, w), lambda i: (i, 0)),
                pl.BlockSpec((SUBLANES, w), lambda i: (jnp.maximum(i * per8 - 1, 0), 0)),
                pl.BlockSpec((SUBLANES, w), lambda i: (jnp.minimum((i + 1) * per8, last8), 0)),
                ] + [full(a) for a in args[3:]]
    return pl.pallas_call(
        functools.partial(_rw_prep_kernel, n_blocks=nb), grid=(nb,),
        in_specs=in_specs, out_specs=(out_blk,) * 9,
        out_shape=(jax.ShapeDtypeStruct((m, RW_W), F32),) * 9,
        compiler_params=_params("parallel"),
    )(*args)


def _scan_kernel(r_ref, k_ref, v_ref, kk_ref, lw_ref, icl_ref, ka_ref, s0_ref, y_ref, send_ref,
                 s_scr, *, reverse, n_chunks):
    c = pl.program_id(1)
    C = RW_CHUNK

    @pl.when(c == 0)
    def _():
        s_scr[...] = s0_ref[0]

    r = r_ref[...]
    v = v_ref[...]
    kk = kk_ref[...]
    lw = lw_ref[...]
    icl = icl_ref[...]
    kd = k_ref[...] * (1.0 + (icl - 1.0) * ka_ref[...])
    a = -kk
    b = kk * icl

    ti = lax.broadcasted_iota(jnp.int32, (C, C), 0)
    si = lax.broadcasted_iota(jnp.int32, (C, C), 1)
    before = (si >= ti) if reverse else (si <= ti)
    cl = jnp.dot(before.astype(F32), lw, precision=HI, preferred_element_type=F32)
    tot = cl[0:1, :] if reverse else cl[C - 1:C, :]
    e_neg = jnp.exp(-cl)
    rt = r * jnp.exp(cl)
    at = a * jnp.exp(cl - lw)
    bt = b * e_neg
    kt = kd * e_neg

    lane = lax.broadcasted_iota(jnp.int32, (C, LANES), 1)
    h0 = lane < RW_N
    zero = jnp.zeros((C, LANES), F32)

    def stack(x):
        return jnp.concatenate([jnp.where(h0, x, zero), jnp.where(h0, zero, x)], axis=0)

    la, lr, vv = stack(at), stack(rt), stack(v)
    lhs = jnp.concatenate([la, lr], axis=0)
    rhs = jnp.concatenate([bt, bt, kt, kt], axis=0)
    sc = lax.dot_general(lhs, rhs, _NT, precision=HI, preferred_element_type=F32)

    row = lax.broadcasted_iota(jnp.int32, (2 * C, 2 * C), 0)
    col = lax.broadcasted_iota(jnp.int32, (2 * C, 2 * C), 1)
    same_head = (row >> _RW_N_BITS) == (col >> _RW_N_BITS)
    strict = same_head & ((col > row) if reverse else (col < row))
    incl = same_head & ((col >= row) if reverse else (col <= row))
    x_ab = jnp.where(strict, sc[0:2 * C, 0:2 * C], 0.0)
    x_ak = jnp.where(strict, sc[0:2 * C, 2 * C:4 * C], 0.0)
    x_rb = jnp.where(incl, sc[2 * C:4 * C, 0:2 * C], 0.0)
    x_rk = jnp.where(incl, sc[2 * C:4 * C, 2 * C:4 * C], 0.0)

    def off_mask(bsz):
        bits = (2 * bsz).bit_length() - 1
        blk = (row >> bits) == (col >> bits)
        lo, hi = (row & (2 * bsz - 1)) < bsz, (col & (2 * bsz - 1)) < bsz
        return blk & ((lo & ~hi) if reverse else (~lo & hi))

    eye = (row == col).astype(F32)
    tinv = eye + jnp.where(off_mask(1), x_ab, 0.0)
    bsz = 2
    while bsz < C:
        xo = jnp.where(off_mask(bsz), x_ab, 0.0)
        tinv = tinv + jnp.dot(jnp.dot(tinv, xo, precision=HI, preferred_element_type=F32), tinv,
                              precision=HI, preferred_element_type=F32)
        bsz *= 2

    s = s_scr[...]
    z = lax.dot_general(la, s, _NT, precision=HI, preferred_element_type=F32)
    u = jnp.dot(tinv, z + jnp.dot(x_ak, vv, precision=HI, preferred_element_type=F32),
                precision=HI, preferred_element_type=F32)
    y = (lax.dot_general(lr, s, _NT, precision=HI, preferred_element_type=F32)
         + jnp.dot(jnp.concatenate([x_rb, x_rk], axis=1), jnp.concatenate([u, vv], axis=0),
                   precision=HI, preferred_element_type=F32))
    y_ref[...] = y[0:C] + y[C:2 * C]
    u_pair = u[0:C] + u[C:2 * C]
    upd = lax.dot_general(jnp.concatenate([u_pair, v], axis=0), jnp.concatenate([bt, kt], axis=0),
                          _TN, precision=HI, preferred_element_type=F32)
    s_new = (s + jnp.where(same_head, upd, 0.0)) * jnp.exp(tot)
    s_scr[...] = s_new

    @pl.when(c == n_chunks - 1)
    def _():
        send_ref[0] = s_new


def rw_scan(r, k, v, kk, lw, icl, k_a, s0, *, reverse):
    t = r.shape[0]
    n_chunks = t // RW_CHUNK
    n_pairs = RW_W // LANES
    if reverse:
        blk = pl.BlockSpec((RW_CHUNK, LANES), lambda p, c: (n_chunks - 1 - c, p))
    else:
        blk = pl.BlockSpec((RW_CHUNK, LANES), lambda p, c: (c, p))
    st = pl.BlockSpec((1, LANES, LANES), lambda p, c: (p, 0, 0))
    return pl.pallas_call(
        functools.partial(_scan_kernel, reverse=reverse, n_chunks=n_chunks),
        grid=(n_pairs, n_chunks),
        in_specs=[blk] * 6 + [pl.BlockSpec((1, LANES), lambda p, c: (0, p)), st],
        out_specs=(blk, st),
        out_shape=(jax.ShapeDtypeStruct((t, RW_W), F32),
                   jax.ShapeDtypeStruct((n_pairs, LANES, LANES), F32)),
        scratch_shapes=[pltpu.VMEM((LANES, LANES), F32)],
        compiler_params=_params("parallel", "arbitrary"),
    )(r, k, v, kk, lw, icl, _row(k_a), s0)


def _rw_post_kernel(yf_ref, yb_ref, r_ref, k_ref, v_ref, iclf_ref, iclb_ref, gate_ref,
                    ka_ref, rk_ref, lnw_ref, lnb_ref, o_ref):
    ones = _head_ones()
    y = yf_ref[...] + yb_ref[...]
    mu = _head_sum(y, ones) * (1.0 / RW_N)
    yc = y - mu
    var = _head_sum(yc * yc, ones) * (1.0 / RW_N)
    yn = yc * lax.rsqrt(var + RW_LN_EPS) * lnw_ref[...] + lnb_ref[...]
    k = k_ref[...]
    ka = ka_ref[...]
    rrk = r_ref[...] * rk_ref[...]
    kd_f = k * (1.0 + (iclf_ref[...] - 1.0) * ka)
    kd_b = k * (1.0 + (iclb_ref[...] - 1.0) * ka)
    bonus = (_head_sum(rrk * kd_f, ones) + _head_sum(rrk * kd_b, ones)) * v_ref[...]
    o_ref[...] = ((yn + bonus) * gate_ref[...]).astype(o_ref.dtype)


def rw_post(yf, yb, r, k, v, icl_f, icl_b, gate, k_a, r_k, ln_w, ln_b, *, tm=256):
    m = yf.shape[0]
    tm = min(tm, m)
    blk = pl.BlockSpec((tm, RW_W), lambda i: (i, 0))
    vec = pl.BlockSpec((1, RW_W), lambda i: (0, 0))
    return pl.pallas_call(
        _rw_post_kernel, grid=(m // tm,),
        in_specs=[blk] * 8 + [vec] * 4, out_specs=blk,
        out_shape=jax.ShapeDtypeStruct((m, RW_W), BF16),
        compiler_params=_params("parallel"),
    )(yf, yb, r, k, v, icl_f, icl_b, gate, _row(k_a), _row(r_k.reshape(-1)), _row(ln_w), _row(ln_b))


def _merge_kernel(ya_ref, yb_ref, yr_ref, yd_ref, zg_ref, bu_ref, gu_ref, gb_ref, o_ref):
    zg = zg_ref[...].astype(BF16)
    acc = None
    for bi, y_ref in enumerate((ya_ref, yb_ref, yr_ref, yd_ref)):
        gate = jax.nn.sigmoid(jnp.dot(zg, gu_ref[bi], preferred_element_type=F32) + gb_ref[bi])
        term = gate * jnp.dot(y_ref[...], bu_ref[bi], preferred_element_type=F32)
        acc = term if acc is None else acc + term
    o_ref[...] = acc.astype(o_ref.dtype)


def merge(ys, zg, branch_up, gate_up, gate_bias, *, tm=1024, tn=512):
    m = zg.shape[0]
    d = branch_up.shape[-1]
    tm, tn = min(tm, m), min(tn, d)
    yblk = pl.BlockSpec((tm, BRANCH_W), lambda i, j: (i, 0))
    return pl.pallas_call(
        _merge_kernel, grid=(m // tm, d // tn),
        in_specs=[yblk] * 4 + [pl.BlockSpec((tm, GATE_R), lambda i, j: (i, 0)),
                               pl.BlockSpec((N_BRANCH, BRANCH_W, tn), lambda i, j: (0, 0, j)),
                               pl.BlockSpec((N_BRANCH, GATE_R, tn), lambda i, j: (0, 0, j)),
                               pl.BlockSpec((N_BRANCH, 1, tn), lambda i, j: (0, 0, j))],
        out_specs=pl.BlockSpec((tm, tn), lambda i, j: (i, j)),
        out_shape=jax.ShapeDtypeStruct((m, d), BF16),
        compiler_params=_params("parallel", "parallel"),
    )(*ys, zg, branch_up, gate_up, gate_bias.reshape(N_BRANCH, 1, d))


def _conv_act_kernel(g_ref, gp_ref, gn_ref, x_ref, xp_ref, xn_ref, wg_ref, wx_ref, o_ref, *, n_blocks):
    def conv(z_ref, zp_ref, zn_ref, w_ref):
        z = z_ref[...]
        prev_row, next_row = _halo_rows(zp_ref, zn_ref, n_blocks)
        zm, zp = _shifted(z, prev_row, next_row)
        return zm * w_ref[0:1, :] + z * w_ref[1:2, :] + zp * w_ref[2:3, :]
    gate = conv(g_ref, gp_ref, gn_ref, wg_ref)
    val = conv(x_ref, xp_ref, xn_ref, wx_ref)
    o_ref[...] = (gate * jax.nn.sigmoid(gate) * val).astype(o_ref.dtype)


def conv_act(hid, w_conv, *, tm=256, tn=1024):
    m, two_f = hid.shape
    f = two_f // 2
    tm, tn = min(tm, m), min(tn, f)
    nb, nj = m // tm, f // tn
    per8 = tm // SUBLANES
    last8 = m // SUBLANES - 1

    def specs(off):
        return [pl.BlockSpec((tm, tn), lambda i, j: (i, j + off)),
                pl.BlockSpec((SUBLANES, tn), lambda i, j: (jnp.maximum(i * per8 - 1, 0), j + off)),
                pl.BlockSpec((SUBLANES, tn), lambda i, j: (jnp.minimum((i + 1) * per8, last8), j + off))]
    wspec = lambda off: pl.BlockSpec((3, tn), lambda i, j: (0, j + off))
    return pl.pallas_call(
        functools.partial(_conv_act_kernel, n_blocks=nb), grid=(nb, nj),
        in_specs=specs(0) + specs(nj) + [wspec(0), wspec(nj)],
        out_specs=pl.BlockSpec((tm, tn), lambda i, j: (i, j)),
        out_shape=jax.ShapeDtypeStruct((m, f), BF16),
        compiler_params=_params("parallel", "parallel"),
    )(hid, hid, hid, hid, hid, hid, w_conv, w_conv)


def diff_mixer(za, zac, lam_vec, subln, lam_init, tabs, need_ctx):
    w = DA_HEADS * HEAD_DIM
    scale = DA_SUB ** -0.5
    q = prep(za, 0, w, rope=tabs, quarter=DA_SUB // 4, scale=scale)
    k = prep(za, 1, w, rope=tabs, quarter=DA_SUB // 4)
    v = prep(za, 2, w)
    qc = prep(zac, 0, w, scale=scale)
    kc = prep(zac, 1, w)
    vc = prep(zac, 2, w)
    kw = dict(mode="diff", lam=lam_vec, subln=subln, lam_init=lam_init)
    y = flash(q, k, v, kc, vc, tq=256, **kw)
    yc = flash(qc, None, None, kc, vc, tq=256, **kw) if need_ctx else None
    return y, yc


def gqa_mixer(zb, zbc, q_gain, k_gain, tabs, need_ctx):
    wq, wk = GQA_HEADS * HEAD_DIM, GQA_KV * HEAD_DIM
    scale = HEAD_DIM ** -0.5
    kcol, vcol = wq // wk, wq // wk + 1
    q = prep(zb, 0, wq, gain=q_gain, rope=tabs, quarter=HEAD_DIM // 4, scale=scale)
    k = prep(zb, kcol, wk, gain=k_gain, rope=tabs, quarter=HEAD_DIM // 4)
    v = prep(zb, vcol, wk)
    qc = prep(zbc, 0, wq, gain=q_gain, scale=scale)
    kc = prep(zbc, kcol, wk, gain=k_gain)
    vc = prep(zbc, vcol, wk)
    y = flash(q, k, v, kc, vc, mode="gqa", tq=128)
    yc = flash(qc, None, None, kc, vc, mode="gqa", tq=128) if need_ctx else None
    return y, yc


def swa_mixer(zd, zdc, sink, tabs, need_ctx):
    wq, wk = SWA_HEADS * HEAD_DIM, SWA_KV * HEAD_DIM
    scale = HEAD_DIM ** -0.5
    kcol, vcol = wq // wk, wq // wk + 1
    q = prep(zd, 0, wq, rope=tabs, quarter=HEAD_DIM // 4, scale=scale)
    k = prep(zd, kcol, wk, rope=tabs, quarter=HEAD_DIM // 4)
    v = prep(zd, vcol, wk)
    qc = prep(zdc, 0, wq, scale=scale)
    kc = prep(zdc, kcol, wk)
    vc = prep(zdc, vcol, wk)
    y = swa(q, k, v, kc, vc, sink)
    yc = flash(qc, None, None, kc, vc, mode="gqa", tq=128, sink=sink) if need_ctx else None
    return y, yc


def rwkv_mixer(zr, zrc, p, need_ctx):
    prep_args = (p["shift"], p["k_k"], p["w0"], p["w_up"], p["a0"], p["a_up"], p["g_up"])
    r, k, v, kk, lwf, lwb, iclf, iclb, gate = rw_prep(zr, *prep_args)
    rc, kc, vc, kkc, lwfc, lwbc, iclfc, iclbc, gatec = rw_prep(zrc, *prep_args)
    s0 = jnp.zeros((RW_W // LANES, LANES, LANES), F32)
    ys, ycs = [], []
    for reverse, lw, icl, lwc, iclc in ((False, lwf, iclf, lwfc, iclfc), (True, lwb, iclb, lwbc, iclbc)):
        yc_d, s_ctx = rw_scan(rc, kc, vc, kkc, lwc, iclc, p["k_a"], s0, reverse=reverse)
        y_d, _ = rw_scan(r, k, v, kk, lw, icl, p["k_a"], s_ctx, reverse=reverse)
        ys.append(y_d)
        ycs.append(yc_d)
    post_args = (p["k_a"], p["r_k"], p["ln_w"], p["ln_b"])
    y = rw_post(ys[0], ys[1], r, k, v, iclf, iclb, gate, *post_args)
    yc = rw_post(ycs[0], ycs[1], rc, kc, vc, iclfc, iclbc, gatec, *post_args) if need_ctx else None
    return y, yc


def _split_cols(w, sizes):
    out, acc = [], 0
    for s in sizes:
        out.append(w[:, acc:acc + s])
        acc += s
    return out


_SLAB_TN = (512, 512, 768, 512, 256)


def kernel(x, c, ctx, c_ctx, mod_down, mod_up, mod_bias, norm_mix_pre, norm_mix_post, norm_ffn_pre, norm_ffn_post, w_in, diff_lambda, diff_subln, gqa_q_norm, gqa_k_norm, rwkv_shift, rwkv_w0, rwkv_w_up, rwkv_a0, rwkv_a_up, rwkv_g_up, rwkv_k_k, rwkv_k_a, rwkv_r_k, rwkv_ln_w, rwkv_ln_b, swa_sink, branch_up, gate_up, gate_bias, w_out, ffn_up, ffn_conv, ffn_down):
    depth = w_in.shape[0]
    s, d = x.shape[1], x.shape[2]
    n_rows = s // GRID_W
    tabs_h = rope_tables(n_rows, HEAD_DIM)
    tabs_s = rope_tables(n_rows, DA_SUB)
    h, hc = x[0], ctx[0]
    cvec = jnp.zeros((16, d), F32).at[0].set(c[0]).at[1].set(c_ctx)
    for l in range(depth):
        need_ctx = l < depth - 1
        lam_init = 0.8 - 0.6 * math.exp(-0.3 * l)
        md = matmul(cvec, mod_down[l].astype(BF16), pre_silu=True, tm=16).astype(BF16)
        mods = matmul(md, mod_up[l].astype(BF16), bias=_row(mod_bias[l]), tm=16, tn=2048)
        m = mods[0].reshape(N_MOD, d)
        mc = mods[1].reshape(N_MOD, d)
        slabs = [w.astype(BF16) for w in _split_cols(w_in[l], IN_SIZES)]
        rw_p = dict(shift=rwkv_shift[l], k_k=rwkv_k_k[l], w0=rwkv_w0[l], w_up=rwkv_w_up[l].astype(BF16),
                    a0=rwkv_a0[l], a_up=rwkv_a_up[l].astype(BF16), g_up=rwkv_g_up[l].astype(BF16),
                    k_a=rwkv_k_a[l], r_k=rwkv_r_k[l], ln_w=rwkv_ln_w[l], ln_b=rwkv_ln_b[l])
        bu, gu = branch_up[l].astype(BF16), gate_up[l].astype(BF16)
        wo, fu, fd = w_out[l].astype(BF16), ffn_up[l].astype(BF16), ffn_down[l].astype(BF16)

        u = rms_mod(h, norm_mix_pre[l], m[0], m[1])
        uc = rms_mod(hc, norm_mix_pre[l], mc[0], mc[1])
        za, zb, zr, zd, zg = [matmul(u, w, tn=tn) for w, tn in zip(slabs, _SLAB_TN)]
        zac, zbc, zrc, zdc, zgc = [matmul(uc, w, tn=tn) for w, tn in zip(slabs, _SLAB_TN)]
        ya, yac = diff_mixer(za, zac, diff_lambda[l], diff_subln[l], lam_init, tabs_s, need_ctx)
        yb, ybc = gqa_mixer(zb, zbc, gqa_q_norm[l], gqa_k_norm[l], tabs_h, need_ctx)
        yr, yrc = rwkv_mixer(zr, zrc, rw_p, need_ctx)
        yd, ydc = swa_mixer(zd, zdc, swa_sink[l], tabs_h, need_ctx)

        def sublayers(hh, ys, zgate, mm, last):
            acc = merge(ys, zgate, bu, gu, gate_bias[l])
            mix = matmul(acc, wo)
            hh, u2 = resid_norm(hh, mix, norm_mix_post[l], mm[2], (norm_ffn_pre[l], mm[3], mm[4]))
            hid = matmul(u2, fu, tn=1024)
            act = conv_act(hid, ffn_conv[l])
            f = matmul(act, fd, tk=2048)
            return resid_norm(hh, f, norm_ffn_post[l], mm[5])

        h = sublayers(h, (ya, yb, yr, yd), zg, m, l == depth - 1)
        if need_ctx:
            hc = sublayers(hc, (yac, ybc, yrc, ydc), zgc, mc, False)
    return h[None]
```

```python
import functools
import math

import jax
import jax.numpy as jnp
from jax import lax
from jax.experimental import pallas as pl
from jax.experimental.pallas import tpu as pltpu

F32 = jnp.float32
BF16 = jnp.bfloat16
HI = lax.Precision.HIGHEST

GRID_W = 64
BLOCK = 128
WINDOW = 128
ROPE_THETA = 10000.0
NORM_EPS = 1e-6
HEAD_DIM = 128
DA_HEADS = 8
DA_SUB = HEAD_DIM // 2
GQA_HEADS = 8
GQA_KV = 2
SWA_HEADS = 8
SWA_KV = 2
RW_HEADS = 16
RW_N = 64
RW_W = RW_HEADS * RW_N
RW_DECAY_R = 128
RW_A_R = 128
RW_GATE_R = 256
RW_LN_EPS = 64e-5
RW_DECAY_SCALE = 0.6065306597126334
N_BRANCH = 4
BRANCH_W = 1024
GATE_R = 256
N_MOD = 6
DA_COLS = 3 * DA_HEADS * HEAD_DIM
GQA_COLS = (GQA_HEADS + 2 * GQA_KV) * HEAD_DIM
RW_COLS = 3 * RW_W + 2 * RW_DECAY_R + 2 * RW_A_R + RW_GATE_R
SWA_COLS = (SWA_HEADS + 2 * SWA_KV) * HEAD_DIM
IN_SIZES = (DA_COLS, GQA_COLS, RW_COLS, SWA_COLS, GATE_R)

LANES = 128
SUBLANES = 8
VMEM_LIMIT = 56 * 1024 * 1024
RW_CHUNK = 64
_RW_N_BITS = RW_N.bit_length() - 1
assert RW_CHUNK == RW_N == 1 << _RW_N_BITS
NEG_BIG = -1e30
LOG2E = math.log2(math.e)

_NT = (((1,), (1,)), ((), ()))
_TN = (((0,), (0,)), ((), ()))


def _params(*sem):
    return pltpu.CompilerParams(dimension_semantics=sem, vmem_limit_bytes=VMEM_LIMIT)


def _row(v):
    return v.reshape(1, -1)


def _mm_kernel(*refs, nk, has_bias, pre_silu):
    a_ref, b_ref = refs[0], refs[1]
    bias_ref = refs[2] if has_bias else None
    o_ref = refs[3] if has_bias else refs[2]
    a = a_ref[...]
    if pre_silu:
        a = (a * jax.nn.sigmoid(a)).astype(BF16)
    prod = jnp.dot(a, b_ref[...], preferred_element_type=F32)

    def finish(acc):
        if has_bias:
            acc = acc + bias_ref[...]
        o_ref[...] = acc.astype(o_ref.dtype)

    if nk == 1:
        finish(prod)
        return
    acc_ref = refs[-1]
    k = pl.program_id(2)

    @pl.when(k == 0)
    def _():
        acc_ref[...] = prod

    @pl.when(k > 0)
    def _():
        acc_ref[...] += prod

    @pl.when(k == nk - 1)
    def _():
        finish(acc_ref[...])


def matmul(a, b, *, out_dtype=F32, tm=1024, tn=512, tk=None, bias=None, pre_silu=False):
    m, kd = a.shape
    n = b.shape[1]
    tm, tn = min(tm, m), min(tn, n)
    tk = kd if tk is None else min(tk, kd)
    assert m % tm == 0 and n % tn == 0 and kd % tk == 0, (a.shape, b.shape, tm, tn, tk)
    nk = kd // tk
    in_specs = [pl.BlockSpec((tm, tk), lambda i, j, k: (i, k)),
                pl.BlockSpec((tk, tn), lambda i, j, k: (k, j))]
    args = [a, b]
    if bias is not None:
        in_specs.append(pl.BlockSpec((1, tn), lambda i, j, k: (0, j)))
        args.append(bias)
    return pl.pallas_call(
        functools.partial(_mm_kernel, nk=nk, has_bias=bias is not None, pre_silu=pre_silu),
        grid=(m // tm, n // tn, nk),
        in_specs=in_specs,
        out_specs=pl.BlockSpec((tm, tn), lambda i, j, k: (i, j)),
        out_shape=jax.ShapeDtypeStruct((m, n), out_dtype),
        scratch_shapes=[pltpu.VMEM((tm, tn), F32)] if nk > 1 else [],
        compiler_params=_params("parallel", "parallel", "arbitrary"),
    )(*args)


def _rms(x, gain):
    return x * lax.rsqrt(jnp.mean(x * x, axis=-1, keepdims=True) + NORM_EPS) * gain


def _rms_mod_kernel(h_ref, g_ref, sh_ref, sc_ref, o_ref):
    y = _rms(h_ref[...], g_ref[...])
    o_ref[...] = (y * (1.0 + sc_ref[...]) + sh_ref[...]).astype(o_ref.dtype)


def rms_mod(h, gain, shift, scale, *, tm=256):
    m, d = h.shape
    tm = min(tm, m)
    vec = pl.BlockSpec((1, d), lambda i: (0, 0))
    return pl.pallas_call(
        _rms_mod_kernel, grid=(m // tm,),
        in_specs=[pl.BlockSpec((tm, d), lambda i: (i, 0)), vec, vec, vec],
        out_specs=pl.BlockSpec((tm, d), lambda i: (i, 0)),
        out_shape=jax.ShapeDtypeStruct((m, d), BF16),
        compiler_params=_params("parallel"),
    )(h, _row(gain), _row(shift), _row(scale))


def _resid_kernel(*refs, with_next):
    h_ref, y_ref, gp_ref, gate_ref = refs[:4]
    h_new = h_ref[...] + gate_ref[...] * _rms(y_ref[...], gp_ref[...])
    if with_next:
        gn_ref, sh_ref, sc_ref, o_ref, u_ref = refs[4:]
        u = _rms(h_new, gn_ref[...])
        u_ref[...] = (u * (1.0 + sc_ref[...]) + sh_ref[...]).astype(u_ref.dtype)
    else:
        o_ref = refs[4]
    o_ref[...] = h_new


def resid_norm(h, y, gain_post, gate, nxt=None, *, tm=256):
    m, d = h.shape
    tm = min(tm, m)
    vec = pl.BlockSpec((1, d), lambda i: (0, 0))
    blk = pl.BlockSpec((tm, d), lambda i: (i, 0))
    args = [h, y, _row(gain_post), _row(gate)]
    in_specs = [blk, blk, vec, vec]
    out_specs, out_shape = blk, jax.ShapeDtypeStruct((m, d), F32)
    if nxt is not None:
        args += [_row(v) for v in nxt]
        in_specs += [vec, vec, vec]
        out_specs = (blk, blk)
        out_shape = (out_shape, jax.ShapeDtypeStruct((m, d), BF16))
    return pl.pallas_call(
        functools.partial(_resid_kernel, with_next=nxt is not None), grid=(m // tm,),
        in_specs=in_specs, out_specs=out_specs, out_shape=out_shape,
        compiler_params=_params("parallel"),
    )(*args)


def rope_tables(n_rows, dim):
    row = jnp.repeat(jnp.arange(n_rows, dtype=F32), GRID_W)
    col = jnp.tile(jnp.arange(GRID_W, dtype=F32), n_rows)
    quarter = dim // 4
    inv_freq = ROPE_THETA ** (-jnp.arange(quarter, dtype=F32) / quarter)
    ang_r = row[:, None] * inv_freq[None, :]
    ang_c = col[:, None] * inv_freq[None, :]
    ang = jnp.concatenate([ang_r, ang_r, ang_c, ang_c], axis=-1)
    cos, sin = jnp.cos(ang), jnp.sin(ang)
    first = (jnp.arange(dim) % (2 * quarter)) < quarter
    sin_p = jnp.where(first[None, :], -sin, 0.0)
    sin_m = jnp.where(first[None, :], 0.0, sin)
    rep = LANES // dim
    return tuple(jnp.tile(t, (1, rep)) for t in (cos, sin_p, sin_m))


def _prep_kernel(*refs, n_groups, has_gain, has_rope, quarter, scale):
    refs = list(refs)
    z_ref = refs.pop(0)
    gain_ref = refs.pop(0) if has_gain else None
    if has_rope:
        cos = refs.pop(0)[...]
        sin_p = refs.pop(0)[...]
        sin_m = refs.pop(0)[...]
    o_ref = refs.pop(0)
    for j in range(n_groups):
        x = z_ref[:, j * LANES:(j + 1) * LANES]
        if has_gain:
            x = _rms(x, gain_ref[...])
        if has_rope:
            x = (x * cos + pltpu.roll(x, LANES - quarter, 1) * sin_p
                 + pltpu.roll(x, quarter, 1) * sin_m)
        if scale != 1.0:
            x = x * scale
        o_ref[:, j * LANES:(j + 1) * LANES] = x.astype(o_ref.dtype)


def prep(z, col_block, width, *, gain=None, rope=None, quarter=0, scale=1.0, tm=256):
    m = z.shape[0]
    tm = min(tm, m)
    args = [z]
    in_specs = [pl.BlockSpec((tm, width), lambda i: (i, col_block))]
    if gain is not None:
        args.append(_row(gain))
        in_specs.append(pl.BlockSpec((1, LANES), lambda i: (0, 0)))
    if rope is not None:
        args += list(rope)
        in_specs += [pl.BlockSpec((tm, LANES), lambda i: (i, 0))] * 3
    return pl.pallas_call(
        functools.partial(_prep_kernel, n_groups=width // LANES, has_gain=gain is not None,
                          has_rope=rope is not None, quarter=quarter, scale=scale),
        grid=(m // tm,), in_specs=in_specs,
        out_specs=pl.BlockSpec((tm, width), lambda i: (i, 0)),
        out_shape=jax.ShapeDtypeStruct((m, width), BF16),
        compiler_params=_params("parallel"),
    )(*args)


def _flash_kernel(*refs, mode, g, tq, tk, n_chunks, has_sink, lam_init):
    refs = list(refs)
    q_ref, k_ref, v_ref, kc_ref, vc_ref = refs[:5]
    rest = refs[5:]
    if mode == "diff":
        lam_ref, subln_ref, o_ref, m_scr, l_scr, acc_scr, sa_scr, sb_scr = rest
    elif has_sink:
        sink_ref, o_ref, m_scr, l_scr, acc_scr, sa_scr, sb_scr = rest
    else:
        o_ref, m_scr, l_scr, acc_scr, sa_scr, sb_scr = rest

    if mode == "diff":
        q = q_ref[...]
        lane = lax.broadcasted_iota(jnp.int32, q.shape, 1)
        zero = jnp.zeros_like(q)
        qs = jnp.concatenate([jnp.where(lane < DA_SUB, q, zero),
                              jnp.where(lane >= DA_SUB, q, zero)], axis=0)
    else:
        qs = jnp.concatenate([q_ref[:, h * LANES:(h + 1) * LANES] for h in range(g)], axis=0)

    sink_col = None
    if has_sink:
        kv = pl.program_id(0)
        sink_col = jnp.concatenate(
            [jnp.full((tq, 1), sink_ref[kv * g + h] * LOG2E, F32) for h in range(g)], axis=0)

    def lane_sums(p):
        parts = [p[:, j * LANES:(j + 1) * LANES] for j in range(p.shape[1] // LANES)]
        while len(parts) > 1:
            parts = [a + b for a, b in zip(parts[0::2], parts[1::2])]
        return parts[0]

    s = lax.dot_general(qs, kc_ref[...], _NT, preferred_element_type=F32)
    m = jnp.max(s, axis=-1, keepdims=True)
    if has_sink:
        m = jnp.maximum(m, sink_col)
    p = jnp.exp2(s - m)
    l_lanes = lane_sums(p)
    acc = jnp.dot(p.astype(BF16), vc_ref[...], preferred_element_type=F32)

    if n_chunks > 0:
        m_scr[...] = m
        l_scr[...] = l_lanes
        acc_scr[...] = acc

        def scores(c):
            off = pl.multiple_of(c * tk, tk)
            return lax.dot_general(qs, k_ref[pl.ds(off, tk), :], _NT, preferred_element_type=F32)

        def update(s_scr, c):
            off = pl.multiple_of(c * tk, tk)
            sb = s_scr[...]
            m_prev = m_scr[...]
            m_new = jnp.maximum(m_prev, jnp.max(sb, axis=-1, keepdims=True))
            alpha = jnp.exp2(m_prev - m_new)
            pb = jnp.exp2(sb - m_new)
            l_scr[...] = alpha * l_scr[...] + lane_sums(pb)
            acc_scr[...] = alpha * acc_scr[...] + jnp.dot(
                pb.astype(BF16), v_ref[pl.ds(off, tk), :], preferred_element_type=F32)
            m_scr[...] = m_new

        sa_scr[...] = scores(0)

        def body(j, carry):
            c0 = 2 * j
            sb_scr[...] = scores(c0 + 1)
            update(sa_scr, c0)
            sa_scr[...] = scores(jnp.minimum(c0 + 2, n_chunks - 1))
            update(sb_scr, c0 + 1)
            return carry

        lax.fori_loop(0, n_chunks // 2, body, 0)
        m, l_lanes, acc = m_scr[...], l_scr[...], acc_scr[...]

    l = jnp.sum(l_lanes, axis=-1, keepdims=True)
    if has_sink:
        l = l + jnp.exp2(sink_col - m)
    out = acc / l
    if mode == "diff":
        lv = lam_ref[...]
        lam = (jnp.exp(jnp.sum(lv[0:1] * lv[1:2], axis=-1, keepdims=True))
               - jnp.exp(jnp.sum(lv[2:3] * lv[3:4], axis=-1, keepdims=True)) + lam_init)
        y = out[:tq] - lam * out[tq:]
        o_ref[...] = (_rms(y, subln_ref[...]) * (1.0 - lam_init)).astype(o_ref.dtype)
    else:
        for h in range(g):
            o_ref[:, h * LANES:(h + 1) * LANES] = out[h * tq:(h + 1) * tq].astype(o_ref.dtype)


def flash(q, k, v, kc, vc, *, mode, tq, tk=1024, sink=None, lam=None, subln=None, lam_init=0.0):
    sq = q.shape[0]
    n_ctx = kc.shape[0]
    if k is None:
        k, v, n_chunks, t_main = kc, vc, 0, n_ctx
    else:
        t_main = k.shape[0]
        tk = min(tk, t_main // 2)
        n_chunks = t_main // tk
        assert n_chunks % 2 == 0 and n_chunks * tk == t_main
    tq = min(tq, sq)
    if mode == "diff":
        n_outer, g, qw = DA_HEADS, 2, LANES
    else:
        n_outer, g = k.shape[1] // LANES, q.shape[1] // k.shape[1]
        qw = g * LANES
    rows = g * tq
    kv_spec = pl.BlockSpec((t_main, LANES), lambda h, i: (0, h))
    ctx_spec = pl.BlockSpec((n_ctx, LANES), lambda h, i: (0, h))
    in_specs = [pl.BlockSpec((tq, qw), lambda h, i: (i, h)), kv_spec, kv_spec, ctx_spec, ctx_spec]
    args = [q, k, v, kc, vc]
    if mode == "diff":
        in_specs += [pl.BlockSpec(lam.shape, lambda h, i: (0, 0)),
                     pl.BlockSpec((1, LANES), lambda h, i: (0, 0))]
        args += [lam, _row(subln)]
    elif sink is not None:
        in_specs.append(pl.BlockSpec(memory_space=pltpu.SMEM))
        args.append(sink)
    return pl.pallas_call(
        functools.partial(_flash_kernel, mode=mode, g=g, tq=tq, tk=tk, n_chunks=n_chunks,
                          has_sink=sink is not None, lam_init=lam_init),
        grid=(n_outer, sq // tq), in_specs=in_specs,
        out_specs=pl.BlockSpec((tq, qw), lambda h, i: (i, h)),
        out_shape=jax.ShapeDtypeStruct((sq, n_outer * qw), BF16),
        scratch_shapes=[pltpu.VMEM((rows, 1), F32), pltpu.VMEM((rows, LANES), F32),
                        pltpu.VMEM((rows, LANES), F32),
                        pltpu.VMEM((rows, tk), F32), pltpu.VMEM((rows, tk), F32)],
        compiler_params=_params("parallel", "arbitrary"),
    )(*args)


def _swa_kernel(q_ref, kp_ref, kn_ref, kx_ref, vp_ref, vn_ref, vx_ref, kc_ref, vc_ref, sink_ref,
                o_ref, *, g, seq):
    kv = pl.program_id(0)
    n = pl.program_id(1)
    qs = jnp.concatenate([q_ref[:, h * LANES:(h + 1) * LANES] for h in range(g)], axis=0)
    kw = jnp.concatenate([kp_ref[...], kn_ref[...], kx_ref[...]], axis=0)
    vw = jnp.concatenate([vp_ref[...], vn_ref[...], vx_ref[...]], axis=0)
    s_win = lax.dot_general(qs, kw, _NT, preferred_element_type=F32)
    rows = g * BLOCK
    start = n * BLOCK
    qpos = start + (lax.broadcasted_iota(jnp.int32, (rows, 3 * BLOCK), 0) & (BLOCK - 1))
    kpos = start - BLOCK + lax.broadcasted_iota(jnp.int32, (rows, 3 * BLOCK), 1)
    ok = (jnp.abs(qpos - kpos) <= WINDOW) & (kpos >= 0) & (kpos < seq)
    s_win = jnp.where(ok, s_win, NEG_BIG)
    s_ctx = lax.dot_general(qs, kc_ref[...], _NT, preferred_element_type=F32)
    sink_col = jnp.concatenate(
        [jnp.full((BLOCK, 1), sink_ref[kv * g + h] * LOG2E, F32) for h in range(g)], axis=0)
    m = jnp.maximum(jnp.maximum(jnp.max(s_win, axis=-1, keepdims=True),
                                jnp.max(s_ctx, axis=-1, keepdims=True)), sink_col)
    p_win = jnp.exp2(s_win - m)
    p_ctx = jnp.exp2(s_ctx - m)
    l = (jnp.sum(p_win, axis=-1, keepdims=True) + jnp.sum(p_ctx, axis=-1, keepdims=True)
         + jnp.exp2(sink_col - m))
    out = (jnp.dot(p_win.astype(BF16), vw, preferred_element_type=F32)
           + jnp.dot(p_ctx.astype(BF16), vc_ref[...], preferred_element_type=F32)) / l
    for h in range(g):
        o_ref[:, h * LANES:(h + 1) * LANES] = out[h * BLOCK:(h + 1) * BLOCK].astype(o_ref.dtype)


def swa(q, k, v, kc, vc, sink):
    seq = q.shape[0]
    nb = seq // BLOCK
    n_kv = k.shape[1] // LANES
    g = q.shape[1] // k.shape[1]
    n_ctx = kc.shape[0]
    prev = pl.BlockSpec((BLOCK, LANES), lambda h, i: (jnp.maximum(i - 1, 0), h))
    cur = pl.BlockSpec((BLOCK, LANES), lambda h, i: (i, h))
    nxt = pl.BlockSpec((BLOCK, LANES), lambda h, i: (jnp.minimum(i + 1, nb - 1), h))
    ctx_spec = pl.BlockSpec((n_ctx, LANES), lambda h, i: (0, h))
    qspec = pl.BlockSpec((BLOCK, g * LANES), lambda h, i: (i, h))
    return pl.pallas_call(
        functools.partial(_swa_kernel, g=g, seq=seq), grid=(n_kv, nb),
        in_specs=[qspec, prev, cur, nxt, prev, cur, nxt, ctx_spec, ctx_spec,
                  pl.BlockSpec(memory_space=pltpu.SMEM)],
        out_specs=qspec,
        out_shape=jax.ShapeDtypeStruct(q.shape, BF16),
        compiler_params=_params("parallel", "parallel"),
    )(q, k, k, k, v, v, v, kc, vc, sink)


def _head_ones():
    r = lax.broadcasted_iota(jnp.int32, (LANES, LANES), 0) >> _RW_N_BITS
    c = lax.broadcasted_iota(jnp.int32, (LANES, LANES), 1) >> _RW_N_BITS
    return (r == c).astype(F32)


def _head_sum(x, ones):
    parts = [jnp.dot(x[:, j * LANES:(j + 1) * LANES], ones, precision=HI,
                     preferred_element_type=F32) for j in range(x.shape[1] // LANES)]
    return jnp.concatenate(parts, axis=1)


def _shifted(z, prev_row, next_row):
    tm = z.shape[0]
    rid = lax.broadcasted_iota(jnp.int32, z.shape, 0)
    zm = jnp.where(rid == 0, prev_row, pltpu.roll(z, 1, 0))
    zp = jnp.where(rid == tm - 1, next_row, pltpu.roll(z, tm - 1, 0))
    return zm, zp


def _halo_rows(zp_ref, zn_ref, n_blocks):
    i = pl.program_id(0)
    prev_row = jnp.where(i > 0, zp_ref[SUBLANES - 1:SUBLANES, :], 0.0)
    next_row = jnp.where(i < n_blocks - 1, zn_ref[0:1, :], 0.0)
    return prev_row, next_row


def _rw_prep_kernel(z_ref, zp_ref, zn_ref, sw_ref, kk_ref, w0_ref, wup_ref, a0_ref, aup_ref,
                    gup_ref, r_o, k_o, v_o, kk_o, lwf_o, lwb_o, iclf_o, iclb_o, gate_o, *, n_blocks):
    z = z_ref[...]
    prev_row, next_row = _halo_rows(zp_ref, zn_ref, n_blocks)
    zm, zp = _shifted(z, prev_row, next_row)
    t = zm * sw_ref[0:1, :] + z * sw_ref[1:2, :] + zp * sw_ref[2:3, :]
    r_o[...] = t[:, 0:RW_W]
    k = t[:, RW_W:2 * RW_W]
    k_o[...] = k
    v_o[...] = t[:, 2 * RW_W:3 * RW_W]
    kk = k * kk_ref[...]
    ss = _head_sum(kk * kk, _head_ones())
    kk_o[...] = kk * lax.rsqrt(jnp.maximum(ss, 1e-24))
    base = 3 * RW_W
    for d, (lw_o, icl_o) in enumerate(((lwf_o, iclf_o), (lwb_o, iclb_o))):
        xw = t[:, base + d * RW_DECAY_R: base + (d + 1) * RW_DECAY_R]
        pre = w0_ref[d:d + 1, :] + jnp.dot(jnp.tanh(xw).astype(BF16), wup_ref[d],
                                           preferred_element_type=F32)
        lw_o[...] = -RW_DECAY_SCALE * jax.nn.sigmoid(pre)
        a_base = base + 2 * RW_DECAY_R
        xa = t[:, a_base + d * RW_A_R: a_base + (d + 1) * RW_A_R]
        icl_o[...] = jax.nn.sigmoid(a0_ref[d:d + 1, :] + jnp.dot(
            xa.astype(BF16), aup_ref[d], preferred_element_type=F32))
    xg = t[:, base + 2 * RW_DECAY_R + 2 * RW_A_R:]
    gate_o[...] = jnp.dot(jax.nn.sigmoid(xg).astype(BF16), gup_ref[...],
                          preferred_element_type=F32)


def rw_prep(zr, shift_w, k_k, w0, w_up, a0, a_up, g_up, *, tm=128):
    m, w = zr.shape
    tm = min(tm, m)
    nb = m // tm
    per8 = tm // SUBLANES
    last8 = m // SUBLANES - 1
    full = lambda a: pl.BlockSpec(a.shape, lambda i: (0,) * a.ndim)
    out_blk = pl.BlockSpec((tm, RW_W), lambda i: (i, 0))
    args = [zr, zr, zr, shift_w, _row(k_k), w0, w_up, a0, a_up, g_up]
    in_specs = [pl.BlockSpec((tm, w), lambda i: (i, 0)),
                pl.BlockSpec((SUBLANES, w), lambda i: (jnp.maximum(i * per8 - 1, 0), 0)),
                pl.BlockSpec((SUBLANES, w), lambda i: (jnp.minimum((i + 1) * per8, last8), 0)),
                ] + [full(a) for a in args[3:]]
    return pl.pallas_call(
        functools.partial(_rw_prep_kernel, n_blocks=nb), grid=(nb,),
        in_specs=in_specs, out_specs=(out_blk,) * 9,
        out_shape=(jax.ShapeDtypeStruct((m, RW_W), F32),) * 9,
        compiler_params=_params("parallel"),
    )(*args)


_MM = (((1,), (0,)), ((), ()))


def _dot1(a, b, dims=_MM):
    return lax.dot_general(a.astype(BF16), b.astype(BF16), dims, preferred_element_type=F32)


def _hilo(x):
    hi = x.astype(BF16)
    return hi, (x - hi.astype(F32)).astype(BF16)


def _dot3(a, b):
    (ah, al), (bh, bl) = _hilo(a), _hilo(b)
    return jnp.dot(jnp.concatenate([ah, ah, al], axis=1), jnp.concatenate([bh, bl, bh], axis=0),
                   preferred_element_type=F32)


def _scan_chunk(r, k, v, kk, lw, icl, ka, states, *, reverse):
    C = RW_CHUNK
    n_pairs = len(states)
    each = range(n_pairs)
    kd = k * (1.0 + (icl - 1.0) * ka)
    a = -kk
    b = kk * icl

    ti = lax.broadcasted_iota(jnp.int32, (C, C), 0)
    si = lax.broadcasted_iota(jnp.int32, (C, C), 1)
    before = (si >= ti) if reverse else (si <= ti)
    cl = jnp.dot(before.astype(F32), lw, precision=HI, preferred_element_type=F32)
    tot = cl[0:1, :] if reverse else cl[C - 1:C, :]
    e_neg = jnp.exp(-cl)
    rt = r * jnp.exp(cl)
    at = a * jnp.exp(cl - lw)
    bt = b * e_neg
    kt = kd * e_neg

    lane = lax.broadcasted_iota(jnp.int32, (C, LANES), 1)
    h0 = lane < RW_N
    zero = jnp.zeros((C, LANES), F32)

    def stack(x):
        return jnp.concatenate([jnp.where(h0, x, zero), jnp.where(h0, zero, x)], axis=0)

    def pair(x, p):
        return x[:, p * LANES:(p + 1) * LANES]

    la = [stack(pair(at, p)) for p in each]
    lr = [stack(pair(rt, p)) for p in each]
    vv = [stack(pair(v, p)) for p in each]
    sc = [_dot1(jnp.concatenate([la[p], lr[p]], axis=0),
                jnp.concatenate([pair(bt, p), pair(bt, p), pair(kt, p), pair(kt, p)], axis=0), _NT)
          for p in each]

    row = lax.broadcasted_iota(jnp.int32, (2 * C, 2 * C), 0)
    col = lax.broadcasted_iota(jnp.int32, (2 * C, 2 * C), 1)
    same_head = (row >> _RW_N_BITS) == (col >> _RW_N_BITS)
    strict = same_head & ((col > row) if reverse else (col < row))
    incl = same_head & ((col >= row) if reverse else (col <= row))
    x_ab = [jnp.where(strict, sc[p][0:2 * C, 0:2 * C], 0.0) for p in each]
    x_ak = [jnp.where(strict, sc[p][0:2 * C, 2 * C:4 * C], 0.0) for p in each]
    x_rb = [jnp.where(incl, sc[p][2 * C:4 * C, 0:2 * C], 0.0) for p in each]
    x_rk = [jnp.where(incl, sc[p][2 * C:4 * C, 2 * C:4 * C], 0.0) for p in each]

    def off_mask(bsz):
        bits = (2 * bsz).bit_length() - 1
        blk = (row >> bits) == (col >> bits)
        lo, hi = (row & (2 * bsz - 1)) < bsz, (col & (2 * bsz - 1)) < bsz
        return blk & ((lo & ~hi) if reverse else (~lo & hi))

    eye = (row == col).astype(F32)
    mask1 = off_mask(1)
    tinv = [eye + jnp.where(mask1, x_ab[p], 0.0) for p in each]
    bsz = 2
    while bsz < C:
        mask = off_mask(bsz)
        half = [_dot3(tinv[p], jnp.where(mask, x_ab[p], 0.0)) for p in each]
        tinv = [tinv[p] + _dot3(half[p], tinv[p]) for p in each]
        bsz *= 2

    z = [_dot1(la[p], states[p], _NT) + _dot1(x_ak[p], vv[p]) for p in each]
    u = [_dot3(tinv[p], z[p]) for p in each]
    y = [_dot1(lr[p], states[p], _NT)
         + _dot1(jnp.concatenate([x_rb[p], x_rk[p]], axis=1), jnp.concatenate([u[p], vv[p]], axis=0))
         for p in each]
    upd = [_dot1(jnp.concatenate([u[p][0:C] + u[p][C:2 * C], pair(v, p)], axis=0),
                 jnp.concatenate([pair(bt, p), pair(kt, p)], axis=0), _TN) for p in each]
    decay = jnp.exp(tot)
    new_states = [(states[p] + jnp.where(same_head, upd[p], 0.0)) * pair(decay, p) for p in each]
    ys = [y[p][0:C] + y[p][C:2 * C] for p in each]
    return ys, new_states


def _scan_kernel(r_ref, k_ref, v_ref, kk_ref, lw_ref, icl_ref, ka_ref, s0_ref, y_ref, send_ref,
                 s_scr, *, reverse, n_chunks, pairs):
    c = pl.program_id(1)

    @pl.when(c == 0)
    def _():
        s_scr[...] = s0_ref[...]

    ys, new_states = _scan_chunk(r_ref[...], k_ref[...], v_ref[...], kk_ref[...], lw_ref[...],
                                 icl_ref[...], ka_ref[...], [s_scr[p] for p in range(pairs)],
                                 reverse=reverse)
    for p in range(pairs):
        y_ref[:, p * LANES:(p + 1) * LANES] = ys[p]
        s_scr[p] = new_states[p]

    @pl.when(c == n_chunks - 1)
    def _():
        send_ref[...] = s_scr[...]


def rw_scan(r, k, v, kk, lw, icl, k_a, s0, *, reverse, pairs=8):
    t = r.shape[0]
    n_chunks = t // RW_CHUNK
    n_pairs = RW_W // LANES
    w = pairs * LANES
    if reverse:
        blk = pl.BlockSpec((RW_CHUNK, w), lambda p, c: (n_chunks - 1 - c, p))
    else:
        blk = pl.BlockSpec((RW_CHUNK, w), lambda p, c: (c, p))
    st = pl.BlockSpec((pairs, LANES, LANES), lambda p, c: (p, 0, 0))
    return pl.pallas_call(
        functools.partial(_scan_kernel, reverse=reverse, n_chunks=n_chunks, pairs=pairs),
        grid=(n_pairs // pairs, n_chunks),
        in_specs=[blk] * 6 + [pl.BlockSpec((1, w), lambda p, c: (0, p)), st],
        out_specs=(blk, st),
        out_shape=(jax.ShapeDtypeStruct((t, RW_W), F32),
                   jax.ShapeDtypeStruct((n_pairs, LANES, LANES), F32)),
        scratch_shapes=[pltpu.VMEM((pairs, LANES, LANES), F32)],
        compiler_params=_params("parallel", "arbitrary"),
    )(r, k, v, kk, lw, icl, _row(k_a), s0)


def _rw_post_kernel(yf_ref, yb_ref, r_ref, k_ref, v_ref, iclf_ref, iclb_ref, gate_ref,
                    ka_ref, rk_ref, lnw_ref, lnb_ref, o_ref):
    ones = _head_ones()
    y = yf_ref[...] + yb_ref[...]
    mu = _head_sum(y, ones) * (1.0 / RW_N)
    yc = y - mu
    var = _head_sum(yc * yc, ones) * (1.0 / RW_N)
    yn = yc * lax.rsqrt(var + RW_LN_EPS) * lnw_ref[...] + lnb_ref[...]
    k = k_ref[...]
    ka = ka_ref[...]
    rrk = r_ref[...] * rk_ref[...]
    kd_f = k * (1.0 + (iclf_ref[...] - 1.0) * ka)
    kd_b = k * (1.0 + (iclb_ref[...] - 1.0) * ka)
    bonus = (_head_sum(rrk * kd_f, ones) + _head_sum(rrk * kd_b, ones)) * v_ref[...]
    o_ref[...] = ((yn + bonus) * gate_ref[...]).astype(o_ref.dtype)


def rw_post(yf, yb, r, k, v, icl_f, icl_b, gate, k_a, r_k, ln_w, ln_b, *, tm=256):
    m = yf.shape[0]
    tm = min(tm, m)
    blk = pl.BlockSpec((tm, RW_W), lambda i: (i, 0))
    vec = pl.BlockSpec((1, RW_W), lambda i: (0, 0))
    return pl.pallas_call(
        _rw_post_kernel, grid=(m // tm,),
        in_specs=[blk] * 8 + [vec] * 4, out_specs=blk,
        out_shape=jax.ShapeDtypeStruct((m, RW_W), BF16),
        compiler_params=_params("parallel"),
    )(yf, yb, r, k, v, icl_f, icl_b, gate, _row(k_a), _row(r_k.reshape(-1)), _row(ln_w), _row(ln_b))


def _merge_kernel(ya_ref, yb_ref, yr_ref, yd_ref, zg_ref, bu_ref, gu_ref, gb_ref, o_ref):
    zg = zg_ref[...].astype(BF16)
    acc = None
    for bi, y_ref in enumerate((ya_ref, yb_ref, yr_ref, yd_ref)):
        gate = jax.nn.sigmoid(jnp.dot(zg, gu_ref[bi], preferred_element_type=F32) + gb_ref[bi])
        term = gate * jnp.dot(y_ref[...], bu_ref[bi], preferred_element_type=F32)
        acc = term if acc is None else acc + term
    o_ref[...] = acc.astype(o_ref.dtype)


def merge(ys, zg, branch_up, gate_up, gate_bias, *, tm=1024, tn=512):
    m = zg.shape[0]
    d = branch_up.shape[-1]
    tm, tn = min(tm, m), min(tn, d)
    yblk = pl.BlockSpec((tm, BRANCH_W), lambda i, j: (i, 0))
    return pl.pallas_call(
        _merge_kernel, grid=(m // tm, d // tn),
        in_specs=[yblk] * 4 + [pl.BlockSpec((tm, GATE_R), lambda i, j: (i, 0)),
                               pl.BlockSpec((N_BRANCH, BRANCH_W, tn), lambda i, j: (0, 0, j)),
                               pl.BlockSpec((N_BRANCH, GATE_R, tn), lambda i, j: (0, 0, j)),
                               pl.BlockSpec((N_BRANCH, 1, tn), lambda i, j: (0, 0, j))],
        out_specs=pl.BlockSpec((tm, tn), lambda i, j: (i, j)),
        out_shape=jax.ShapeDtypeStruct((m, d), BF16),
        compiler_params=_params("parallel", "parallel"),
    )(*ys, zg, branch_up, gate_up, gate_bias.reshape(N_BRANCH, 1, d))


def _conv_act_kernel(g_ref, gp_ref, gn_ref, x_ref, xp_ref, xn_ref, wg_ref, wx_ref, o_ref, *, n_blocks):
    def conv(z_ref, zp_ref, zn_ref, w_ref):
        z = z_ref[...]
        prev_row, next_row = _halo_rows(zp_ref, zn_ref, n_blocks)
        zm, zp = _shifted(z, prev_row, next_row)
        return zm * w_ref[0:1, :] + z * w_ref[1:2, :] + zp * w_ref[2:3, :]
    gate = conv(g_ref, gp_ref, gn_ref, wg_ref)
    val = conv(x_ref, xp_ref, xn_ref, wx_ref)
    o_ref[...] = (gate * jax.nn.sigmoid(gate) * val).astype(o_ref.dtype)


def conv_act(hid, w_conv, *, tm=256, tn=1024):
    m, two_f = hid.shape
    f = two_f // 2
    tm, tn = min(tm, m), min(tn, f)
    nb, nj = m // tm, f // tn
    per8 = tm // SUBLANES
    last8 = m // SUBLANES - 1

    def specs(off):
        return [pl.BlockSpec((tm, tn), lambda i, j: (i, j + off)),
                pl.BlockSpec((SUBLANES, tn), lambda i, j: (jnp.maximum(i * per8 - 1, 0), j + off)),
                pl.BlockSpec((SUBLANES, tn), lambda i, j: (jnp.minimum((i + 1) * per8, last8), j + off))]
    wspec = lambda off: pl.BlockSpec((3, tn), lambda i, j: (0, j + off))
    return pl.pallas_call(
        functools.partial(_conv_act_kernel, n_blocks=nb), grid=(nb, nj),
        in_specs=specs(0) + specs(nj) + [wspec(0), wspec(nj)],
        out_specs=pl.BlockSpec((tm, tn), lambda i, j: (i, j)),
        out_shape=jax.ShapeDtypeStruct((m, f), BF16),
        compiler_params=_params("parallel", "parallel"),
    )(hid, hid, hid, hid, hid, hid, w_conv, w_conv)


def diff_mixer(za, zac, lam_vec, subln, lam_init, tabs, need_ctx):
    w = DA_HEADS * HEAD_DIM
    scale = DA_SUB ** -0.5 * LOG2E
    q = prep(za, 0, w, rope=tabs, quarter=DA_SUB // 4, scale=scale)
    k = prep(za, 1, w, rope=tabs, quarter=DA_SUB // 4)
    v = prep(za, 2, w)
    qc = prep(zac, 0, w, scale=scale)
    kc = prep(zac, 1, w)
    vc = prep(zac, 2, w)
    kw = dict(mode="diff", lam=lam_vec, subln=subln, lam_init=lam_init)
    y = flash(q, k, v, kc, vc, tq=256, **kw)
    yc = flash(qc, None, None, kc, vc, tq=256, **kw) if need_ctx else None
    return y, yc


def gqa_mixer(zb, zbc, q_gain, k_gain, tabs, need_ctx):
    wq, wk = GQA_HEADS * HEAD_DIM, GQA_KV * HEAD_DIM
    scale = HEAD_DIM ** -0.5 * LOG2E
    kcol, vcol = wq // wk, wq // wk + 1
    q = prep(zb, 0, wq, gain=q_gain, rope=tabs, quarter=HEAD_DIM // 4, scale=scale)
    k = prep(zb, kcol, wk, gain=k_gain, rope=tabs, quarter=HEAD_DIM // 4)
    v = prep(zb, vcol, wk)
    qc = prep(zbc, 0, wq, gain=q_gain, scale=scale)
    kc = prep(zbc, kcol, wk, gain=k_gain)
    vc = prep(zbc, vcol, wk)
    y = flash(q, k, v, kc, vc, mode="gqa", tq=128)
    yc = flash(qc, None, None, kc, vc, mode="gqa", tq=128) if need_ctx else None
    return y, yc


def swa_mixer(zd, zdc, sink, tabs, need_ctx):
    wq, wk = SWA_HEADS * HEAD_DIM, SWA_KV * HEAD_DIM
    scale = HEAD_DIM ** -0.5 * LOG2E
    kcol, vcol = wq // wk, wq // wk + 1
    q = prep(zd, 0, wq, rope=tabs, quarter=HEAD_DIM // 4, scale=scale)
    k = prep(zd, kcol, wk, rope=tabs, quarter=HEAD_DIM // 4)
    v = prep(zd, vcol, wk)
    qc = prep(zdc, 0, wq, scale=scale)
    kc = prep(zdc, kcol, wk)
    vc = prep(zdc, vcol, wk)
    y = swa(q, k, v, kc, vc, sink)
    yc = flash(qc, None, None, kc, vc, mode="gqa", tq=128, sink=sink) if need_ctx else None
    return y, yc


def rwkv_mixer(zr, zrc, p, need_ctx):
    prep_args = (p["shift"], p["k_k"], p["w0"], p["w_up"], p["a0"], p["a_up"], p["g_up"])
    r, k, v, kk, lwf, lwb, iclf, iclb, gate = rw_prep(zr, *prep_args)
    rc, kc, vc, kkc, lwfc, lwbc, iclfc, iclbc, gatec = rw_prep(zrc, *prep_args)
    s0 = jnp.zeros((RW_W // LANES, LANES, LANES), F32)
    ys, ycs = [], []
    for reverse, lw, icl, lwc, iclc in ((False, lwf, iclf, lwfc, iclfc), (True, lwb, iclb, lwbc, iclbc)):
        yc_d, s_ctx = rw_scan(rc, kc, vc, kkc, lwc, iclc, p["k_a"], s0, reverse=reverse)
        y_d, _ = rw_scan(r, k, v, kk, lw, icl, p["k_a"], s_ctx, reverse=reverse)
        ys.append(y_d)
        ycs.append(yc_d)
    post_args = (p["k_a"], p["r_k"], p["ln_w"], p["ln_b"])
    y = rw_post(ys[0], ys[1], r, k, v, iclf, iclb, gate, *post_args)
    yc = rw_post(ycs[0], ycs[1], rc, kc, vc, iclfc, iclbc, gatec, *post_args) if need_ctx else None
    return y, yc


def _split_cols(w, sizes):
    out, acc = [], 0
    for s in sizes:
        out.append(w[:, acc:acc + s])
        acc += s
    return out


_SLAB_TN = (512, 512, 768, 512, 256)


def kernel(x, c, ctx, c_ctx, mod_down, mod_up, mod_bias, norm_mix_pre, norm_mix_post, norm_ffn_pre, norm_ffn_post, w_in, diff_lambda, diff_subln, gqa_q_norm, gqa_k_norm, rwkv_shift, rwkv_w0, rwkv_w_up, rwkv_a0, rwkv_a_up, rwkv_g_up, rwkv_k_k, rwkv_k_a, rwkv_r_k, rwkv_ln_w, rwkv_ln_b, swa_sink, branch_up, gate_up, gate_bias, w_out, ffn_up, ffn_conv, ffn_down):
    depth = w_in.shape[0]
    s, d = x.shape[1], x.shape[2]
    n_rows = s // GRID_W
    tabs_h = rope_tables(n_rows, HEAD_DIM)
    tabs_s = rope_tables(n_rows, DA_SUB)
    h, hc = x[0], ctx[0]
    cvec = jnp.zeros((16, d), F32).at[0].set(c[0]).at[1].set(c_ctx)
    for l in range(depth):
        need_ctx = l < depth - 1
        lam_init = 0.8 - 0.6 * math.exp(-0.3 * l)
        md = matmul(cvec, mod_down[l].astype(BF16), pre_silu=True, tm=16).astype(BF16)
        mods = matmul(md, mod_up[l].astype(BF16), bias=_row(mod_bias[l]), tm=16, tn=2048)
        m = mods[0].reshape(N_MOD, d)
        mc = mods[1].reshape(N_MOD, d)
        slabs = [w.astype(BF16) for w in _split_cols(w_in[l], IN_SIZES)]
        rw_p = dict(shift=rwkv_shift[l], k_k=rwkv_k_k[l], w0=rwkv_w0[l], w_up=rwkv_w_up[l].astype(BF16),
                    a0=rwkv_a0[l], a_up=rwkv_a_up[l].astype(BF16), g_up=rwkv_g_up[l].astype(BF16),
                    k_a=rwkv_k_a[l], r_k=rwkv_r_k[l], ln_w=rwkv_ln_w[l], ln_b=rwkv_ln_b[l])
        bu, gu = branch_up[l].astype(BF16), gate_up[l].astype(BF16)
        wo, fu, fd = w_out[l].astype(BF16), ffn_up[l].astype(BF16), ffn_down[l].astype(BF16)

        u = rms_mod(h, norm_mix_pre[l], m[0], m[1])
        uc = rms_mod(hc, norm_mix_pre[l], mc[0], mc[1])
        za, zb, zr, zd, zg = [matmul(u, w, tn=tn) for w, tn in zip(slabs, _SLAB_TN)]
        zac, zbc, zrc, zdc, zgc = [matmul(uc, w, tn=tn) for w, tn in zip(slabs, _SLAB_TN)]
        ya, yac = diff_mixer(za, zac, diff_lambda[l], diff_subln[l], lam_init, tabs_s, need_ctx)
        yb, ybc = gqa_mixer(zb, zbc, gqa_q_norm[l], gqa_k_norm[l], tabs_h, need_ctx)
        yr, yrc = rwkv_mixer(zr, zrc, rw_p, need_ctx)
        yd, ydc = swa_mixer(zd, zdc, swa_sink[l], tabs_h, need_ctx)

        def sublayers(hh, ys, zgate, mm, last):
            acc = merge(ys, zgate, bu, gu, gate_bias[l])
            mix = matmul(acc, wo)
            hh, u2 = resid_norm(hh, mix, norm_mix_post[l], mm[2], (norm_ffn_pre[l], mm[3], mm[4]))
            hid = matmul(u2, fu, tn=1024)
            act = conv_act(hid, ffn_conv[l])
            f = matmul(act, fd, tk=2048)
            return resid_norm(hh, f, norm_ffn_post[l], mm[5])

        h = sublayers(h, (ya, yb, yr, yd), zg, m, l == depth - 1)
        if need_ctx:
            hc = sublayers(hc, (yac, ybc, yrc, ydc), zgc, mc, False)
    return h[None]
```

```python
import functools
import math

import jax
import jax.numpy as jnp
from jax import lax
from jax.experimental import pallas as pl
from jax.experimental.pallas import tpu as pltpu

F32 = jnp.float32
BF16 = jnp.bfloat16
HI = lax.Precision.HIGHEST

GRID_W = 64
BLOCK = 128
WINDOW = 128
ROPE_THETA = 10000.0
NORM_EPS = 1e-6
HEAD_DIM = 128
DA_HEADS = 8
DA_SUB = HEAD_DIM // 2
GQA_HEADS = 8
GQA_KV = 2
SWA_HEADS = 8
SWA_KV = 2
RW_HEADS = 16
RW_N = 64
RW_W = RW_HEADS * RW_N
RW_DECAY_R = 128
RW_A_R = 128
RW_GATE_R = 256
RW_LN_EPS = 64e-5
RW_DECAY_SCALE = 0.6065306597126334
N_BRANCH = 4
BRANCH_W = 1024
GATE_R = 256
N_MOD = 6
DA_COLS = 3 * DA_HEADS * HEAD_DIM
GQA_COLS = (GQA_HEADS + 2 * GQA_KV) * HEAD_DIM
RW_COLS = 3 * RW_W + 2 * RW_DECAY_R + 2 * RW_A_R + RW_GATE_R
SWA_COLS = (SWA_HEADS + 2 * SWA_KV) * HEAD_DIM
IN_SIZES = (DA_COLS, GQA_COLS, RW_COLS, SWA_COLS, GATE_R)

LANES = 128
SUBLANES = 8
VMEM_LIMIT = 56 * 1024 * 1024
RW_CHUNK = 64
_RW_N_BITS = RW_N.bit_length() - 1
assert RW_CHUNK == RW_N == 1 << _RW_N_BITS
NEG_BIG = -1e30
LOG2E = math.log2(math.e)

_NT = (((1,), (1,)), ((), ()))
_TN = (((0,), (0,)), ((), ()))


def _params(*sem):
    return pltpu.CompilerParams(dimension_semantics=sem, vmem_limit_bytes=VMEM_LIMIT)


def _row(v):
    return v.reshape(1, -1)


def _mm_kernel(*refs, nk, has_bias, pre_silu):
    a_ref, b_ref = refs[0], refs[1]
    bias_ref = refs[2] if has_bias else None
    o_ref = refs[3] if has_bias else refs[2]
    a = a_ref[...]
    if pre_silu:
        a = (a * jax.nn.sigmoid(a)).astype(BF16)
    prod = jnp.dot(a, b_ref[...], preferred_element_type=F32)

    def finish(acc):
        if has_bias:
            acc = acc + bias_ref[...]
        o_ref[...] = acc.astype(o_ref.dtype)

    if nk == 1:
        finish(prod)
        return
    acc_ref = refs[-1]
    k = pl.program_id(2)

    @pl.when(k == 0)
    def _():
        acc_ref[...] = prod

    @pl.when(k > 0)
    def _():
        acc_ref[...] += prod

    @pl.when(k == nk - 1)
    def _():
        finish(acc_ref[...])


def matmul(a, b, *, out_dtype=F32, tm=1024, tn=512, tk=None, bias=None, pre_silu=False):
    m, kd = a.shape
    n = b.shape[1]
    tm, tn = min(tm, m), min(tn, n)
    tk = kd if tk is None else min(tk, kd)
    assert m % tm == 0 and n % tn == 0 and kd % tk == 0, (a.shape, b.shape, tm, tn, tk)
    nk = kd // tk
    in_specs = [pl.BlockSpec((tm, tk), lambda i, j, k: (i, k)),
                pl.BlockSpec((tk, tn), lambda i, j, k: (k, j))]
    args = [a, b]
    if bias is not None:
        in_specs.append(pl.BlockSpec((1, tn), lambda i, j, k: (0, j)))
        args.append(bias)
    return pl.pallas_call(
        functools.partial(_mm_kernel, nk=nk, has_bias=bias is not None, pre_silu=pre_silu),
        grid=(m // tm, n // tn, nk),
        in_specs=in_specs,
        out_specs=pl.BlockSpec((tm, tn), lambda i, j, k: (i, j)),
        out_shape=jax.ShapeDtypeStruct((m, n), out_dtype),
        scratch_shapes=[pltpu.VMEM((tm, tn), F32)] if nk > 1 else [],
        compiler_params=_params("parallel", "parallel", "arbitrary"),
    )(*args)


def _mmw_kernel(a_ref, w_ref, o_ref, wb_scr):
    @pl.when(pl.program_id(1) == 0)
    def _():
        wb_scr[...] = w_ref[...].astype(BF16)

    o_ref[...] = jnp.dot(a_ref[...], wb_scr[...], preferred_element_type=F32).astype(o_ref.dtype)


def matmul_w(a, w, layer, col0, n, *, out_dtype=F32, tm=1024, tn=512):
    m, kd = a.shape
    tm, tn = min(tm, m), min(tn, n)
    assert m % tm == 0 and n % tn == 0 and col0 % tn == 0 and w.shape[1] == kd
    cb = col0 // tn
    return pl.pallas_call(
        _mmw_kernel, grid=(n // tn, m // tm),
        in_specs=[pl.BlockSpec((tm, kd), lambda j, i: (i, 0)),
                  pl.BlockSpec((None, kd, tn), lambda j, i: (layer, 0, cb + j))],
        out_specs=pl.BlockSpec((tm, tn), lambda j, i: (i, j)),
        out_shape=jax.ShapeDtypeStruct((m, n), out_dtype),
        scratch_shapes=[pltpu.VMEM((kd, tn), BF16)],
        compiler_params=_params("parallel", "arbitrary"),
    )(a, w)


def _rms(x, gain):
    return x * lax.rsqrt(jnp.mean(x * x, axis=-1, keepdims=True) + NORM_EPS) * gain


def _rms_mod_kernel(h_ref, g_ref, sh_ref, sc_ref, o_ref):
    y = _rms(h_ref[...], g_ref[...])
    o_ref[...] = (y * (1.0 + sc_ref[...]) + sh_ref[...]).astype(o_ref.dtype)


def rms_mod(h, gain, shift, scale, *, tm=256):
    m, d = h.shape
    tm = min(tm, m)
    vec = pl.BlockSpec((1, d), lambda i: (0, 0))
    return pl.pallas_call(
        _rms_mod_kernel, grid=(m // tm,),
        in_specs=[pl.BlockSpec((tm, d), lambda i: (i, 0)), vec, vec, vec],
        out_specs=pl.BlockSpec((tm, d), lambda i: (i, 0)),
        out_shape=jax.ShapeDtypeStruct((m, d), BF16),
        compiler_params=_params("parallel"),
    )(h, _row(gain), _row(shift), _row(scale))


def _resid_kernel(*refs, with_next):
    h_ref, y_ref, gp_ref, gate_ref = refs[:4]
    h_new = h_ref[...] + gate_ref[...] * _rms(y_ref[...], gp_ref[...])
    if with_next:
        gn_ref, sh_ref, sc_ref, o_ref, u_ref = refs[4:]
        u = _rms(h_new, gn_ref[...])
        u_ref[...] = (u * (1.0 + sc_ref[...]) + sh_ref[...]).astype(u_ref.dtype)
    else:
        o_ref = refs[4]
    o_ref[...] = h_new


def resid_norm(h, y, gain_post, gate, nxt=None, *, tm=256):
    m, d = h.shape
    tm = min(tm, m)
    vec = pl.BlockSpec((1, d), lambda i: (0, 0))
    blk = pl.BlockSpec((tm, d), lambda i: (i, 0))
    args = [h, y, _row(gain_post), _row(gate)]
    in_specs = [blk, blk, vec, vec]
    out_specs, out_shape = blk, jax.ShapeDtypeStruct((m, d), F32)
    if nxt is not None:
        args += [_row(v) for v in nxt]
        in_specs += [vec, vec, vec]
        out_specs = (blk, blk)
        out_shape = (out_shape, jax.ShapeDtypeStruct((m, d), BF16))
    return pl.pallas_call(
        functools.partial(_resid_kernel, with_next=nxt is not None), grid=(m // tm,),
        in_specs=in_specs, out_specs=out_specs, out_shape=out_shape,
        compiler_params=_params("parallel"),
    )(*args)


def rope_tables(n_rows, dim):
    row = jnp.repeat(jnp.arange(n_rows, dtype=F32), GRID_W)
    col = jnp.tile(jnp.arange(GRID_W, dtype=F32), n_rows)
    quarter = dim // 4
    inv_freq = ROPE_THETA ** (-jnp.arange(quarter, dtype=F32) / quarter)
    ang_r = row[:, None] * inv_freq[None, :]
    ang_c = col[:, None] * inv_freq[None, :]
    ang = jnp.concatenate([ang_r, ang_r, ang_c, ang_c], axis=-1)
    cos, sin = jnp.cos(ang), jnp.sin(ang)
    first = (jnp.arange(dim) % (2 * quarter)) < quarter
    sin_p = jnp.where(first[None, :], -sin, 0.0)
    sin_m = jnp.where(first[None, :], 0.0, sin)
    rep = LANES // dim
    return tuple(jnp.tile(t, (1, rep)) for t in (cos, sin_p, sin_m))


def _prep_kernel(*refs, n_groups, has_gain, has_rope, quarter, scale):
    refs = list(refs)
    z_ref = refs.pop(0)
    gain_ref = refs.pop(0) if has_gain else None
    if has_rope:
        cos = refs.pop(0)[...]
        sin_p = refs.pop(0)[...]
        sin_m = refs.pop(0)[...]
    o_ref = refs.pop(0)
    for j in range(n_groups):
        x = z_ref[:, j * LANES:(j + 1) * LANES]
        if has_gain:
            x = _rms(x, gain_ref[...])
        if has_rope:
            x = (x * cos + pltpu.roll(x, LANES - quarter, 1) * sin_p
                 + pltpu.roll(x, quarter, 1) * sin_m)
        if scale != 1.0:
            x = x * scale
        o_ref[:, j * LANES:(j + 1) * LANES] = x.astype(o_ref.dtype)


def prep(z, col_block, width, *, gain=None, rope=None, quarter=0, scale=1.0, tm=256):
    m = z.shape[0]
    tm = min(tm, m)
    args = [z]
    in_specs = [pl.BlockSpec((tm, width), lambda i: (i, col_block))]
    if gain is not None:
        args.append(_row(gain))
        in_specs.append(pl.BlockSpec((1, LANES), lambda i: (0, 0)))
    if rope is not None:
        args += list(rope)
        in_specs += [pl.BlockSpec((tm, LANES), lambda i: (i, 0))] * 3
    return pl.pallas_call(
        functools.partial(_prep_kernel, n_groups=width // LANES, has_gain=gain is not None,
                          has_rope=rope is not None, quarter=quarter, scale=scale),
        grid=(m // tm,), in_specs=in_specs,
        out_specs=pl.BlockSpec((tm, width), lambda i: (i, 0)),
        out_shape=jax.ShapeDtypeStruct((m, width), BF16),
        compiler_params=_params("parallel"),
    )(*args)


def _flash_kernel(*refs, mode, g, tq, tk, n_chunks, has_sink, lam_init):
    refs = list(refs)
    q_ref, k_ref, v_ref, kc_ref, vc_ref = refs[:5]
    rest = refs[5:]
    if mode == "diff":
        lam_ref, subln_ref, o_ref, m_scr, l_scr, acc_scr, sa_scr, sb_scr = rest
    elif has_sink:
        sink_ref, o_ref, m_scr, l_scr, acc_scr, sa_scr, sb_scr = rest
    else:
        o_ref, m_scr, l_scr, acc_scr, sa_scr, sb_scr = rest

    if mode == "diff":
        q = q_ref[...]
        lane = lax.broadcasted_iota(jnp.int32, q.shape, 1)
        zero = jnp.zeros_like(q)
        qs = jnp.concatenate([jnp.where(lane < DA_SUB, q, zero),
                              jnp.where(lane >= DA_SUB, q, zero)], axis=0)
    else:
        qs = jnp.concatenate([q_ref[:, h * LANES:(h + 1) * LANES] for h in range(g)], axis=0)

    sink_col = None
    if has_sink:
        kv = pl.program_id(0)
        sink_col = jnp.concatenate(
            [jnp.full((tq, 1), sink_ref[kv * g + h] * LOG2E, F32) for h in range(g)], axis=0)

    def lane_sums(p):
        parts = [p[:, j * LANES:(j + 1) * LANES] for j in range(p.shape[1] // LANES)]
        while len(parts) > 1:
            parts = [a + b for a, b in zip(parts[0::2], parts[1::2])]
        return parts[0]

    s = lax.dot_general(qs, kc_ref[...], _NT, preferred_element_type=F32)
    m = jnp.max(s, axis=-1, keepdims=True)
    if has_sink:
        m = jnp.maximum(m, sink_col)
    p = jnp.exp2(s - m)
    l_lanes = lane_sums(p)
    acc = jnp.dot(p.astype(BF16), vc_ref[...], preferred_element_type=F32)

    if n_chunks > 0:
        m_scr[...] = m
        l_scr[...] = l_lanes
        acc_scr[...] = acc

        def scores(c):
            off = pl.multiple_of(c * tk, tk)
            return lax.dot_general(qs, k_ref[pl.ds(off, tk), :], _NT, preferred_element_type=F32)

        def update(s_scr, c):
            off = pl.multiple_of(c * tk, tk)
            sb = s_scr[...]
            m_prev = m_scr[...]
            m_new = jnp.maximum(m_prev, jnp.max(sb, axis=-1, keepdims=True))
            alpha = jnp.exp2(m_prev - m_new)
            pb = jnp.exp2(sb - m_new)
            l_scr[...] = alpha * l_scr[...] + lane_sums(pb)
            acc_scr[...] = alpha * acc_scr[...] + jnp.dot(
                pb.astype(BF16), v_ref[pl.ds(off, tk), :], preferred_element_type=F32)
            m_scr[...] = m_new

        sa_scr[...] = scores(0)

        def body(j, carry):
            c0 = 2 * j
            sb_scr[...] = scores(c0 + 1)
            update(sa_scr, c0)
            sa_scr[...] = scores(jnp.minimum(c0 + 2, n_chunks - 1))
            update(sb_scr, c0 + 1)
            return carry

        lax.fori_loop(0, n_chunks // 2, body, 0)
        m, l_lanes, acc = m_scr[...], l_scr[...], acc_scr[...]

    l = jnp.sum(l_lanes, axis=-1, keepdims=True)
    if has_sink:
        l = l + jnp.exp2(sink_col - m)
    out = acc / l
    if mode == "diff":
        lv = lam_ref[...]
        lam = (jnp.exp(jnp.sum(lv[0:1] * lv[1:2], axis=-1, keepdims=True))
               - jnp.exp(jnp.sum(lv[2:3] * lv[3:4], axis=-1, keepdims=True)) + lam_init)
        y = out[:tq] - lam * out[tq:]
        o_ref[...] = (_rms(y, subln_ref[...]) * (1.0 - lam_init)).astype(o_ref.dtype)
    else:
        for h in range(g):
            o_ref[:, h * LANES:(h + 1) * LANES] = out[h * tq:(h + 1) * tq].astype(o_ref.dtype)


def flash(q, k, v, kc, vc, *, mode, tq, tk=1024, sink=None, lam=None, subln=None, lam_init=0.0):
    sq = q.shape[0]
    n_ctx = kc.shape[0]
    if k is None:
        k, v, n_chunks, t_main = kc, vc, 0, n_ctx
    else:
        t_main = k.shape[0]
        tk = min(tk, t_main // 2)
        n_chunks = t_main // tk
        assert n_chunks % 2 == 0 and n_chunks * tk == t_main
    tq = min(tq, sq)
    if mode == "diff":
        n_outer, g, qw = DA_HEADS, 2, LANES
    else:
        n_outer, g = k.shape[1] // LANES, q.shape[1] // k.shape[1]
        qw = g * LANES
    rows = g * tq
    kv_spec = pl.BlockSpec((t_main, LANES), lambda h, i: (0, h))
    ctx_spec = pl.BlockSpec((n_ctx, LANES), lambda h, i: (0, h))
    in_specs = [pl.BlockSpec((tq, qw), lambda h, i: (i, h)), kv_spec, kv_spec, ctx_spec, ctx_spec]
    args = [q, k, v, kc, vc]
    if mode == "diff":
        in_specs += [pl.BlockSpec(lam.shape, lambda h, i: (0, 0)),
                     pl.BlockSpec((1, LANES), lambda h, i: (0, 0))]
        args += [lam, _row(subln)]
    elif sink is not None:
        in_specs.append(pl.BlockSpec(memory_space=pltpu.SMEM))
        args.append(sink)
    return pl.pallas_call(
        functools.partial(_flash_kernel, mode=mode, g=g, tq=tq, tk=tk, n_chunks=n_chunks,
                          has_sink=sink is not None, lam_init=lam_init),
        grid=(n_outer, sq // tq), in_specs=in_specs,
        out_specs=pl.BlockSpec((tq, qw), lambda h, i: (i, h)),
        out_shape=jax.ShapeDtypeStruct((sq, n_outer * qw), BF16),
        scratch_shapes=[pltpu.VMEM((rows, 1), F32), pltpu.VMEM((rows, LANES), F32),
                        pltpu.VMEM((rows, LANES), F32),
                        pltpu.VMEM((rows, tk), F32), pltpu.VMEM((rows, tk), F32)],
        compiler_params=_params("parallel", "arbitrary"),
    )(*args)


def _swa_kernel(q_ref, kp_ref, kn_ref, kx_ref, vp_ref, vn_ref, vx_ref, kc_ref, vc_ref, sink_ref,
                o_ref, *, g, seq):
    kv = pl.program_id(0)
    n = pl.program_id(1)
    qs = jnp.concatenate([q_ref[:, h * LANES:(h + 1) * LANES] for h in range(g)], axis=0)
    kw = jnp.concatenate([kp_ref[...], kn_ref[...], kx_ref[...]], axis=0)
    vw = jnp.concatenate([vp_ref[...], vn_ref[...], vx_ref[...]], axis=0)
    s_win = lax.dot_general(qs, kw, _NT, preferred_element_type=F32)
    rows = g * BLOCK
    start = n * BLOCK
    qpos = start + (lax.broadcasted_iota(jnp.int32, (rows, 3 * BLOCK), 0) & (BLOCK - 1))
    kpos = start - BLOCK + lax.broadcasted_iota(jnp.int32, (rows, 3 * BLOCK), 1)
    ok = (jnp.abs(qpos - kpos) <= WINDOW) & (kpos >= 0) & (kpos < seq)
    s_win = jnp.where(ok, s_win, NEG_BIG)
    s_ctx = lax.dot_general(qs, kc_ref[...], _NT, preferred_element_type=F32)
    sink_col = jnp.concatenate(
        [jnp.full((BLOCK, 1), sink_ref[kv * g + h] * LOG2E, F32) for h in range(g)], axis=0)
    m = jnp.maximum(jnp.maximum(jnp.max(s_win, axis=-1, keepdims=True),
                                jnp.max(s_ctx, axis=-1, keepdims=True)), sink_col)
    p_win = jnp.exp2(s_win - m)
    p_ctx = jnp.exp2(s_ctx - m)
    l = (jnp.sum(p_win, axis=-1, keepdims=True) + jnp.sum(p_ctx, axis=-1, keepdims=True)
         + jnp.exp2(sink_col - m))
    out = (jnp.dot(p_win.astype(BF16), vw, preferred_element_type=F32)
           + jnp.dot(p_ctx.astype(BF16), vc_ref[...], preferred_element_type=F32)) / l
    for h in range(g):
        o_ref[:, h * LANES:(h + 1) * LANES] = out[h * BLOCK:(h + 1) * BLOCK].astype(o_ref.dtype)


def swa(q, k, v, kc, vc, sink):
    seq = q.shape[0]
    nb = seq // BLOCK
    n_kv = k.shape[1] // LANES
    g = q.shape[1] // k.shape[1]
    n_ctx = kc.shape[0]
    prev = pl.BlockSpec((BLOCK, LANES), lambda h, i: (jnp.maximum(i - 1, 0), h))
    cur = pl.BlockSpec((BLOCK, LANES), lambda h, i: (i, h))
    nxt = pl.BlockSpec((BLOCK, LANES), lambda h, i: (jnp.minimum(i + 1, nb - 1), h))
    ctx_spec = pl.BlockSpec((n_ctx, LANES), lambda h, i: (0, h))
    qspec = pl.BlockSpec((BLOCK, g * LANES), lambda h, i: (i, h))
    return pl.pallas_call(
        functools.partial(_swa_kernel, g=g, seq=seq), grid=(n_kv, nb),
        in_specs=[qspec, prev, cur, nxt, prev, cur, nxt, ctx_spec, ctx_spec,
                  pl.BlockSpec(memory_space=pltpu.SMEM)],
        out_specs=qspec,
        out_shape=jax.ShapeDtypeStruct(q.shape, BF16),
        compiler_params=_params("parallel", "parallel"),
    )(q, k, k, k, v, v, v, kc, vc, sink)


def _head_ones():
    r = lax.broadcasted_iota(jnp.int32, (LANES, LANES), 0) >> _RW_N_BITS
    c = lax.broadcasted_iota(jnp.int32, (LANES, LANES), 1) >> _RW_N_BITS
    ones = (r == c).astype(BF16)
    return jnp.concatenate([ones, ones, ones], axis=0)


def _head_sum(x, ones3):
    hi = x.astype(BF16)
    rest = x - hi.astype(F32)
    mid = rest.astype(BF16)
    lo = (rest - mid.astype(F32)).astype(BF16)
    parts = []
    for j in range(x.shape[1] // LANES):
        sl = slice(j * LANES, (j + 1) * LANES)
        parts.append(jnp.dot(jnp.concatenate([hi[:, sl], mid[:, sl], lo[:, sl]], axis=1), ones3,
                             preferred_element_type=F32))
    return jnp.concatenate(parts, axis=1)


def _shifted(z, prev_row, next_row):
    tm = z.shape[0]
    rid = lax.broadcasted_iota(jnp.int32, (SUBLANES, z.shape[1]), 0)
    zm, zp = pltpu.roll(z, 1, 0), pltpu.roll(z, tm - 1, 0)
    zm = jnp.concatenate([jnp.where(rid == 0, prev_row, zm[:SUBLANES]), zm[SUBLANES:]], axis=0)
    zp = jnp.concatenate([zp[:tm - SUBLANES],
                          jnp.where(rid == SUBLANES - 1, next_row, zp[tm - SUBLANES:])], axis=0)
    return zm, zp


def _halo_rows(zp_ref, zn_ref, n_blocks):
    i = pl.program_id(0)
    prev_row = jnp.where(i > 0, zp_ref[SUBLANES - 1:SUBLANES, :], 0.0)
    next_row = jnp.where(i < n_blocks - 1, zn_ref[0:1, :], 0.0)
    return prev_row, next_row


def _rw_prep_kernel(z_ref, zp_ref, zn_ref, sw_ref, kk_ref, w0_ref, wup_ref, a0_ref, aup_ref,
                    gup_ref, r_o, k_o, v_o, kk_o, lwf_o, lwb_o, iclf_o, iclb_o, gate_o, *, n_blocks):
    z = z_ref[...]
    prev_row, next_row = _halo_rows(zp_ref, zn_ref, n_blocks)
    zm, zp = _shifted(z, prev_row, next_row)
    t = zm * sw_ref[0:1, :] + z * sw_ref[1:2, :] + zp * sw_ref[2:3, :]
    r_o[...] = t[:, 0:RW_W]
    k = t[:, RW_W:2 * RW_W]
    k_o[...] = k
    v_o[...] = t[:, 2 * RW_W:3 * RW_W]
    kk = k * kk_ref[...]
    ss = _head_sum(kk * kk, _head_ones())
    kk_o[...] = kk * lax.rsqrt(jnp.maximum(ss, 1e-24))
    base = 3 * RW_W
    for d, (lw_o, icl_o) in enumerate(((lwf_o, iclf_o), (lwb_o, iclb_o))):
        xw = t[:, base + d * RW_DECAY_R: base + (d + 1) * RW_DECAY_R]
        pre = w0_ref[d:d + 1, :] + jnp.dot(jnp.tanh(xw).astype(BF16), wup_ref[d],
                                           preferred_element_type=F32)
        lw_o[...] = -RW_DECAY_SCALE * jax.nn.sigmoid(pre)
        a_base = base + 2 * RW_DECAY_R
        xa = t[:, a_base + d * RW_A_R: a_base + (d + 1) * RW_A_R]
        icl_o[...] = jax.nn.sigmoid(a0_ref[d:d + 1, :] + jnp.dot(
            xa.astype(BF16), aup_ref[d], preferred_element_type=F32))
    xg = t[:, base + 2 * RW_DECAY_R + 2 * RW_A_R:]
    gate_o[...] = jnp.dot(jax.nn.sigmoid(xg).astype(BF16), gup_ref[...],
                          preferred_element_type=F32)


def rw_prep(zr, shift_w, k_k, w0, w_up, a0, a_up, g_up, *, tm=128):
    m, w = zr.shape
    tm = min(tm, m)
    nb = m // tm
    per8 = tm // SUBLANES
    last8 = m // SUBLANES - 1
    full = lambda a: pl.BlockSpec(a.shape, lambda i: (0,) * a.ndim)
    out_blk = pl.BlockSpec((tm, RW_W), lambda i: (i, 0))
    args = [zr, zr, zr, shift_w, _row(k_k), w0, w_up, a0, a_up, g_up]
    in_specs = [pl.BlockSpec((tm, w), lambda i: (i, 0)),
                pl.BlockSpec((SUBLANES, w), lambda i: (jnp.maximum(i * per8 - 1, 0), 0)),
                pl.BlockSpec((SUBLANES, w), lambda i: (jnp.minimum((i + 1) * per8, last8), 0)),
                ] + [full(a) for a in args[3:]]
    return pl.pallas_call(
        functools.partial(_rw_prep_kernel, n_blocks=nb), grid=(nb,),
        in_specs=in_specs, out_specs=(out_blk,) * 9,
        out_shape=(jax.ShapeDtypeStruct((m, RW_W), F32),) * 9,
        compiler_params=_params("parallel"),
    )(*args)


_MM = (((1,), (0,)), ((), ()))


def _dot1(a, b, dims=_MM):
    return lax.dot_general(a.astype(BF16), b.astype(BF16), dims, preferred_element_type=F32)


def _hilo(x):
    hi = x.astype(BF16)
    return hi, (x - hi.astype(F32)).astype(BF16)


def _dot2(a, b):
    ah = a.astype(BF16)
    bh, bl = _hilo(b)
    return jnp.dot(jnp.concatenate([ah, ah], axis=1), jnp.concatenate([bh, bl], axis=0),
                   preferred_element_type=F32)


def _scan_chunk(r, k, v, kk, lw, icl, ka, states, *, reverse):
    C = RW_CHUNK
    n_pairs = len(states)
    each = range(n_pairs)
    kd = k * (1.0 + (icl - 1.0) * ka)
    a = -kk
    b = kk * icl

    ti = lax.broadcasted_iota(jnp.int32, (C, C), 0)
    si = lax.broadcasted_iota(jnp.int32, (C, C), 1)
    before = (si >= ti) if reverse else (si <= ti)
    cl = jnp.dot(before.astype(F32), lw, precision=HI, preferred_element_type=F32)
    tot = cl[0:1, :] if reverse else cl[C - 1:C, :]
    e_neg = jnp.exp(-cl)
    rt = r * jnp.exp(cl)
    at = a * jnp.exp(cl - lw)
    bt = b * e_neg
    kt = kd * e_neg

    lane = lax.broadcasted_iota(jnp.int32, (C, LANES), 1)
    h0 = lane < RW_N
    zero = jnp.zeros((C, LANES), F32)

    def stack(x):
        return jnp.concatenate([jnp.where(h0, x, zero), jnp.where(h0, zero, x)], axis=0)

    def pair(x, p):
        return x[:, p * LANES:(p + 1) * LANES]

    la = [stack(pair(at, p)) for p in each]
    lr = [stack(pair(rt, p)) for p in each]
    vv = [stack(pair(v, p)) for p in each]
    sc = [_dot1(jnp.concatenate([la[p], lr[p]], axis=0),
                jnp.concatenate([pair(bt, p), pair(bt, p), pair(kt, p), pair(kt, p)], axis=0), _NT)
          for p in each]

    row = lax.broadcasted_iota(jnp.int32, (2 * C, 2 * C), 0)
    col = lax.broadcasted_iota(jnp.int32, (2 * C, 2 * C), 1)
    same_head = (row >> _RW_N_BITS) == (col >> _RW_N_BITS)
    strict = same_head & ((col > row) if reverse else (col < row))
    incl = same_head & ((col >= row) if reverse else (col <= row))
    x_ab = [jnp.where(strict, sc[p][0:2 * C, 0:2 * C], 0.0) for p in each]
    x_ak = [jnp.where(strict, sc[p][0:2 * C, 2 * C:4 * C], 0.0) for p in each]
    x_rb = [jnp.where(incl, sc[p][2 * C:4 * C, 0:2 * C], 0.0) for p in each]
    x_rk = [jnp.where(incl, sc[p][2 * C:4 * C, 2 * C:4 * C], 0.0) for p in each]

    def off_mask(bsz):
        bits = (2 * bsz).bit_length() - 1
        blk = (row >> bits) == (col >> bits)
        lo, hi = (row & (2 * bsz - 1)) < bsz, (col & (2 * bsz - 1)) < bsz
        return blk & ((lo & ~hi) if reverse else (~lo & hi))

    eye = (row == col).astype(F32)
    mask1 = off_mask(1)
    tinv = [eye + jnp.where(mask1, x_ab[p], 0.0) for p in each]
    bsz = 2
    while bsz < C:
        mask = off_mask(bsz)
        half = [_dot2(tinv[p], jnp.where(mask, x_ab[p], 0.0)) for p in each]
        tinv = [tinv[p] + _dot2(half[p], tinv[p]) for p in each]
        bsz *= 2

    z = [_dot1(la[p], states[p], _NT) + _dot1(x_ak[p], vv[p]) for p in each]
    u = [_dot2(tinv[p], z[p]) for p in each]
    y = [_dot1(lr[p], states[p], _NT)
         + _dot1(jnp.concatenate([x_rb[p], x_rk[p]], axis=1), jnp.concatenate([u[p], vv[p]], axis=0))
         for p in each]
    upd = [_dot1(jnp.concatenate([u[p][0:C] + u[p][C:2 * C], pair(v, p)], axis=0),
                 jnp.concatenate([pair(bt, p), pair(kt, p)], axis=0), _TN) for p in each]
    decay = jnp.exp(tot)
    new_states = [(states[p] + jnp.where(same_head, upd[p], 0.0)) * pair(decay, p) for p in each]
    ys = [y[p][0:C] + y[p][C:2 * C] for p in each]
    return ys, new_states


def _scan_kernel(r_ref, k_ref, v_ref, kk_ref, lw_ref, icl_ref, ka_ref, s0_ref, y_ref, send_ref,
                 s_scr, *, reverse, n_chunks, pairs):
    c = pl.program_id(1)

    @pl.when(c == 0)
    def _():
        s_scr[...] = s0_ref[...]

    ys, new_states = _scan_chunk(r_ref[...], k_ref[...], v_ref[...], kk_ref[...], lw_ref[...],
                                 icl_ref[...], ka_ref[...], [s_scr[p] for p in range(pairs)],
                                 reverse=reverse)
    for p in range(pairs):
        y_ref[:, p * LANES:(p + 1) * LANES] = ys[p]
        s_scr[p] = new_states[p]

    @pl.when(c == n_chunks - 1)
    def _():
        send_ref[...] = s_scr[...]


def rw_scan(r, k, v, kk, lw, icl, k_a, s0, *, reverse, pairs=8):
    t = r.shape[0]
    n_chunks = t // RW_CHUNK
    n_pairs = RW_W // LANES
    w = pairs * LANES
    if reverse:
        blk = pl.BlockSpec((RW_CHUNK, w), lambda p, c: (n_chunks - 1 - c, p))
    else:
        blk = pl.BlockSpec((RW_CHUNK, w), lambda p, c: (c, p))
    st = pl.BlockSpec((pairs, LANES, LANES), lambda p, c: (p, 0, 0))
    return pl.pallas_call(
        functools.partial(_scan_kernel, reverse=reverse, n_chunks=n_chunks, pairs=pairs),
        grid=(n_pairs // pairs, n_chunks),
        in_specs=[blk] * 6 + [pl.BlockSpec((1, w), lambda p, c: (0, p)), st],
        out_specs=(blk, st),
        out_shape=(jax.ShapeDtypeStruct((t, RW_W), F32),
                   jax.ShapeDtypeStruct((n_pairs, LANES, LANES), F32)),
        scratch_shapes=[pltpu.VMEM((pairs, LANES, LANES), F32)],
        compiler_params=_params("parallel", "arbitrary"),
    )(r, k, v, kk, lw, icl, _row(k_a), s0)


def _rw_post_kernel(yf_ref, yb_ref, r_ref, k_ref, v_ref, iclf_ref, iclb_ref, gate_ref,
                    ka_ref, rk_ref, lnw_ref, lnb_ref, o_ref):
    ones = _head_ones()
    y = yf_ref[...] + yb_ref[...]
    mu = _head_sum(y, ones) * (1.0 / RW_N)
    yc = y - mu
    var = _head_sum(yc * yc, ones) * (1.0 / RW_N)
    yn = yc * lax.rsqrt(var + RW_LN_EPS) * lnw_ref[...] + lnb_ref[...]
    k = k_ref[...]
    ka = ka_ref[...]
    rrk = r_ref[...] * rk_ref[...]
    kd_f = k * (1.0 + (iclf_ref[...] - 1.0) * ka)
    kd_b = k * (1.0 + (iclb_ref[...] - 1.0) * ka)
    bonus = (_head_sum(rrk * kd_f, ones) + _head_sum(rrk * kd_b, ones)) * v_ref[...]
    o_ref[...] = ((yn + bonus) * gate_ref[...]).astype(o_ref.dtype)


def rw_post(yf, yb, r, k, v, icl_f, icl_b, gate, k_a, r_k, ln_w, ln_b, *, tm=256):
    m = yf.shape[0]
    tm = min(tm, m)
    blk = pl.BlockSpec((tm, RW_W), lambda i: (i, 0))
    vec = pl.BlockSpec((1, RW_W), lambda i: (0, 0))
    return pl.pallas_call(
        _rw_post_kernel, grid=(m // tm,),
        in_specs=[blk] * 8 + [vec] * 4, out_specs=blk,
        out_shape=jax.ShapeDtypeStruct((m, RW_W), BF16),
        compiler_params=_params("parallel"),
    )(yf, yb, r, k, v, icl_f, icl_b, gate, _row(k_a), _row(r_k.reshape(-1)), _row(ln_w), _row(ln_b))


def _merge_kernel(ya_ref, yb_ref, yr_ref, yd_ref, zg_ref, bu_ref, gu_ref, gb_ref, o_ref):
    zg = zg_ref[...].astype(BF16)
    acc = None
    for bi, y_ref in enumerate((ya_ref, yb_ref, yr_ref, yd_ref)):
        gate = jax.nn.sigmoid(jnp.dot(zg, gu_ref[bi], preferred_element_type=F32) + gb_ref[bi])
        term = gate * jnp.dot(y_ref[...], bu_ref[bi], preferred_element_type=F32)
        acc = term if acc is None else acc + term
    o_ref[...] = acc.astype(o_ref.dtype)


def merge(ys, zg, branch_up, gate_up, gate_bias, *, tm=1024, tn=512):
    m = zg.shape[0]
    d = branch_up.shape[-1]
    tm, tn = min(tm, m), min(tn, d)
    yblk = pl.BlockSpec((tm, BRANCH_W), lambda i, j: (i, 0))
    return pl.pallas_call(
        _merge_kernel, grid=(m // tm, d // tn),
        in_specs=[yblk] * 4 + [pl.BlockSpec((tm, GATE_R), lambda i, j: (i, 0)),
                               pl.BlockSpec((N_BRANCH, BRANCH_W, tn), lambda i, j: (0, 0, j)),
                               pl.BlockSpec((N_BRANCH, GATE_R, tn), lambda i, j: (0, 0, j)),
                               pl.BlockSpec((N_BRANCH, 1, tn), lambda i, j: (0, 0, j))],
        out_specs=pl.BlockSpec((tm, tn), lambda i, j: (i, j)),
        out_shape=jax.ShapeDtypeStruct((m, d), BF16),
        compiler_params=_params("parallel", "parallel"),
    )(*ys, zg, branch_up, gate_up, gate_bias.reshape(N_BRANCH, 1, d))


def _conv_act_kernel(g_ref, gp_ref, gn_ref, x_ref, xp_ref, xn_ref, wg_ref, wx_ref, o_ref, *, n_blocks):
    def conv(z_ref, zp_ref, zn_ref, w_ref):
        z = z_ref[...]
        prev_row, next_row = _halo_rows(zp_ref, zn_ref, n_blocks)
        zm, zp = _shifted(z, prev_row, next_row)
        return zm * w_ref[0:1, :] + z * w_ref[1:2, :] + zp * w_ref[2:3, :]
    gate = conv(g_ref, gp_ref, gn_ref, wg_ref)
    val = conv(x_ref, xp_ref, xn_ref, wx_ref)
    o_ref[...] = (gate * jax.nn.sigmoid(gate) * val).astype(o_ref.dtype)


def conv_act(hid, w_conv, *, tm=256, tn=1024):
    m, two_f = hid.shape
    f = two_f // 2
    tm, tn = min(tm, m), min(tn, f)
    nb, nj = m // tm, f // tn
    per8 = tm // SUBLANES
    last8 = m // SUBLANES - 1

    def specs(off):
        return [pl.BlockSpec((tm, tn), lambda i, j: (i, j + off)),
                pl.BlockSpec((SUBLANES, tn), lambda i, j: (jnp.maximum(i * per8 - 1, 0), j + off)),
                pl.BlockSpec((SUBLANES, tn), lambda i, j: (jnp.minimum((i + 1) * per8, last8), j + off))]
    wspec = lambda off: pl.BlockSpec((3, tn), lambda i, j: (0, j + off))
    return pl.pallas_call(
        functools.partial(_conv_act_kernel, n_blocks=nb), grid=(nb, nj),
        in_specs=specs(0) + specs(nj) + [wspec(0), wspec(nj)],
        out_specs=pl.BlockSpec((tm, tn), lambda i, j: (i, j)),
        out_shape=jax.ShapeDtypeStruct((m, f), BF16),
        compiler_params=_params("parallel", "parallel"),
    )(hid, hid, hid, hid, hid, hid, w_conv, w_conv)


def diff_mixer(za, zac, lam_vec, subln, lam_init, tabs, need_ctx):
    w = DA_HEADS * HEAD_DIM
    scale = DA_SUB ** -0.5 * LOG2E
    q = prep(za, 0, w, rope=tabs, quarter=DA_SUB // 4, scale=scale)
    k = prep(za, 1, w, rope=tabs, quarter=DA_SUB // 4)
    v = prep(za, 2, w)
    qc = prep(zac, 0, w, scale=scale)
    kc = prep(zac, 1, w)
    vc = prep(zac, 2, w)
    kw = dict(mode="diff", lam=lam_vec, subln=subln, lam_init=lam_init)
    y = flash(q, k, v, kc, vc, tq=512, **kw)
    yc = flash(qc, None, None, kc, vc, tq=256, **kw) if need_ctx else None
    return y, yc


def gqa_mixer(zb, zbc, q_gain, k_gain, tabs, need_ctx):
    wq, wk = GQA_HEADS * HEAD_DIM, GQA_KV * HEAD_DIM
    scale = HEAD_DIM ** -0.5 * LOG2E
    kcol, vcol = wq // wk, wq // wk + 1
    q = prep(zb, 0, wq, gain=q_gain, rope=tabs, quarter=HEAD_DIM // 4, scale=scale)
    k = prep(zb, kcol, wk, gain=k_gain, rope=tabs, quarter=HEAD_DIM // 4)
    v = prep(zb, vcol, wk)
    qc = prep(zbc, 0, wq, gain=q_gain, scale=scale)
    kc = prep(zbc, kcol, wk, gain=k_gain)
    vc = prep(zbc, vcol, wk)
    y = flash(q, k, v, kc, vc, mode="gqa", tq=256)
    yc = flash(qc, None, None, kc, vc, mode="gqa", tq=128) if need_ctx else None
    return y, yc


def swa_mixer(zd, zdc, sink, tabs, need_ctx):
    wq, wk = SWA_HEADS * HEAD_DIM, SWA_KV * HEAD_DIM
    scale = HEAD_DIM ** -0.5 * LOG2E
    kcol, vcol = wq // wk, wq // wk + 1
    q = prep(zd, 0, wq, rope=tabs, quarter=HEAD_DIM // 4, scale=scale)
    k = prep(zd, kcol, wk, rope=tabs, quarter=HEAD_DIM // 4)
    v = prep(zd, vcol, wk)
    qc = prep(zdc, 0, wq, scale=scale)
    kc = prep(zdc, kcol, wk)
    vc = prep(zdc, vcol, wk)
    y = swa(q, k, v, kc, vc, sink)
    yc = flash(qc, None, None, kc, vc, mode="gqa", tq=128, sink=sink) if need_ctx else None
    return y, yc


def rwkv_mixer(zr, zrc, p, need_ctx):
    prep_args = (p["shift"], p["k_k"], p["w0"], p["w_up"], p["a0"], p["a_up"], p["g_up"])
    r, k, v, kk, lwf, lwb, iclf, iclb, gate = rw_prep(zr, *prep_args)
    rc, kc, vc, kkc, lwfc, lwbc, iclfc, iclbc, gatec = rw_prep(zrc, *prep_args)
    s0 = jnp.zeros((RW_W // LANES, LANES, LANES), F32)
    ys, ycs = [], []
    for reverse, lw, icl, lwc, iclc in ((False, lwf, iclf, lwfc, iclfc), (True, lwb, iclb, lwbc, iclbc)):
        yc_d, s_ctx = rw_scan(rc, kc, vc, kkc, lwc, iclc, p["k_a"], s0, reverse=reverse)
        y_d, _ = rw_scan(r, k, v, kk, lw, icl, p["k_a"], s_ctx, reverse=reverse)
        ys.append(y_d)
        ycs.append(yc_d)
    post_args = (p["k_a"], p["r_k"], p["ln_w"], p["ln_b"])
    y = rw_post(ys[0], ys[1], r, k, v, iclf, iclb, gate, *post_args)
    yc = rw_post(ycs[0], ycs[1], rc, kc, vc, iclfc, iclbc, gatec, *post_args) if need_ctx else None
    return y, yc


_SLAB_TILES = ((1024, 512), (1024, 512), (512, 768), (512, 768), (1024, 256))


def kernel(x, c, ctx, c_ctx, mod_down, mod_up, mod_bias, norm_mix_pre, norm_mix_post, norm_ffn_pre, norm_ffn_post, w_in, diff_lambda, diff_subln, gqa_q_norm, gqa_k_norm, rwkv_shift, rwkv_w0, rwkv_w_up, rwkv_a0, rwkv_a_up, rwkv_g_up, rwkv_k_k, rwkv_k_a, rwkv_r_k, rwkv_ln_w, rwkv_ln_b, swa_sink, branch_up, gate_up, gate_bias, w_out, ffn_up, ffn_conv, ffn_down):
    depth = w_in.shape[0]
    s, d = x.shape[1], x.shape[2]
    n_rows = s // GRID_W
    tabs_h = rope_tables(n_rows, HEAD_DIM)
    tabs_s = rope_tables(n_rows, DA_SUB)
    h, hc = x[0], ctx[0]
    cvec = jnp.zeros((16, d), F32).at[0].set(c[0]).at[1].set(c_ctx)
    for l in range(depth):
        need_ctx = l < depth - 1
        lam_init = 0.8 - 0.6 * math.exp(-0.3 * l)
        md = matmul(cvec, mod_down[l].astype(BF16), pre_silu=True, tm=16).astype(BF16)
        mods = matmul(md, mod_up[l].astype(BF16), bias=_row(mod_bias[l]), tm=16, tn=2048)
        m = mods[0].reshape(N_MOD, d)
        mc = mods[1].reshape(N_MOD, d)
        rw_p = dict(shift=rwkv_shift[l], k_k=rwkv_k_k[l], w0=rwkv_w0[l], w_up=rwkv_w_up[l].astype(BF16),
                    a0=rwkv_a0[l], a_up=rwkv_a_up[l].astype(BF16), g_up=rwkv_g_up[l].astype(BF16),
                    k_a=rwkv_k_a[l], r_k=rwkv_r_k[l], ln_w=rwkv_ln_w[l], ln_b=rwkv_ln_b[l])
        bu, gu = branch_up[l].astype(BF16), gate_up[l].astype(BF16)
        fd = ffn_down[l].astype(BF16)

        def in_proj(a):
            outs, col0 = [], 0
            for width, (tm, tn) in zip(IN_SIZES, _SLAB_TILES):
                outs.append(matmul_w(a, w_in, l, col0, width, tm=tm, tn=tn))
                col0 += width
            return outs

        u = rms_mod(h, norm_mix_pre[l], m[0], m[1])
        uc = rms_mod(hc, norm_mix_pre[l], mc[0], mc[1])
        za, zb, zr, zd, zg = in_proj(u)
        zac, zbc, zrc, zdc, zgc = in_proj(uc)
        ya, yac = diff_mixer(za, zac, diff_lambda[l], diff_subln[l], lam_init, tabs_s, need_ctx)
        yb, ybc = gqa_mixer(zb, zbc, gqa_q_norm[l], gqa_k_norm[l], tabs_h, need_ctx)
        yr, yrc = rwkv_mixer(zr, zrc, rw_p, need_ctx)
        yd, ydc = swa_mixer(zd, zdc, swa_sink[l], tabs_h, need_ctx)

        def sublayers(hh, ys, zgate, mm, last):
            acc = merge(ys, zgate, bu, gu, gate_bias[l])
            mix = matmul_w(acc, w_out, l, 0, d)
            hh, u2 = resid_norm(hh, mix, norm_mix_post[l], mm[2], (norm_ffn_pre[l], mm[3], mm[4]))
            hid = matmul_w(u2, ffn_up, l, 0, ffn_up.shape[2], tm=512, tn=1024)
            act = conv_act(hid, ffn_conv[l])
            f = matmul(act, fd, tm=512)
            return resid_norm(hh, f, norm_ffn_post[l], mm[5])

        h = sublayers(h, (ya, yb, yr, yd), zg, m, l == depth - 1)
        if need_ctx:
            hc = sublayers(hc, (yac, ybc, yrc, ydc), zgc, mc, False)
    return h[None]
```

```python
import functools
import math

import jax
import jax.numpy as jnp
from jax import lax
from jax.experimental import pallas as pl
from jax.experimental.pallas import tpu as pltpu

F32 = jnp.float32
BF16 = jnp.bfloat16
HI = lax.Precision.HIGHEST

GRID_W = 64
BLOCK = 128
WINDOW = 128
ROPE_THETA = 10000.0
NORM_EPS = 1e-6
HEAD_DIM = 128
DA_HEADS = 8
DA_SUB = HEAD_DIM // 2
GQA_HEADS = 8
GQA_KV = 2
SWA_HEADS = 8
SWA_KV = 2
RW_HEADS = 16
RW_N = 64
RW_W = RW_HEADS * RW_N
RW_DECAY_R = 128
RW_A_R = 128
RW_GATE_R = 256
RW_LN_EPS = 64e-5
RW_DECAY_SCALE = 0.6065306597126334
N_BRANCH = 4
BRANCH_W = 1024
GATE_R = 256
N_MOD = 6
DA_COLS = 3 * DA_HEADS * HEAD_DIM
GQA_COLS = (GQA_HEADS + 2 * GQA_KV) * HEAD_DIM
RW_COLS = 3 * RW_W + 2 * RW_DECAY_R + 2 * RW_A_R + RW_GATE_R
SWA_COLS = (SWA_HEADS + 2 * SWA_KV) * HEAD_DIM
IN_SIZES = (DA_COLS, GQA_COLS, RW_COLS, SWA_COLS, GATE_R)

LANES = 128
SUBLANES = 8
VMEM_LIMIT = 56 * 1024 * 1024
RW_CHUNK = 64
_RW_N_BITS = RW_N.bit_length() - 1
assert RW_CHUNK == RW_N == 1 << _RW_N_BITS
NEG_BIG = -1e30
LOG2E = math.log2(math.e)

_NT = (((1,), (1,)), ((), ()))
_TN = (((0,), (0,)), ((), ()))


def _params(*sem):
    return pltpu.CompilerParams(dimension_semantics=sem, vmem_limit_bytes=VMEM_LIMIT)


def _row(v):
    return v.reshape(1, -1)


def _mm_kernel(*refs, nk, has_bias, pre_silu):
    a_ref, b_ref = refs[0], refs[1]
    bias_ref = refs[2] if has_bias else None
    o_ref = refs[3] if has_bias else refs[2]
    a = a_ref[...]
    if pre_silu:
        a = (a * jax.nn.sigmoid(a)).astype(BF16)
    prod = jnp.dot(a, b_ref[...], preferred_element_type=F32)

    def finish(acc):
        if has_bias:
            acc = acc + bias_ref[...]
        o_ref[...] = acc.astype(o_ref.dtype)

    if nk == 1:
        finish(prod)
        return
    acc_ref = refs[-1]
    k = pl.program_id(2)

    @pl.when(k == 0)
    def _():
        acc_ref[...] = prod

    @pl.when(k > 0)
    def _():
        acc_ref[...] += prod

    @pl.when(k == nk - 1)
    def _():
        finish(acc_ref[...])


def matmul(a, b, *, out_dtype=F32, tm=1024, tn=512, tk=None, bias=None, pre_silu=False):
    m, kd = a.shape
    n = b.shape[1]
    tm, tn = min(tm, m), min(tn, n)
    tk = kd if tk is None else min(tk, kd)
    assert m % tm == 0 and n % tn == 0 and kd % tk == 0, (a.shape, b.shape, tm, tn, tk)
    nk = kd // tk
    in_specs = [pl.BlockSpec((tm, tk), lambda i, j, k: (i, k)),
                pl.BlockSpec((tk, tn), lambda i, j, k: (k, j))]
    args = [a, b]
    if bias is not None:
        in_specs.append(pl.BlockSpec((1, tn), lambda i, j, k: (0, j)))
        args.append(bias)
    return pl.pallas_call(
        functools.partial(_mm_kernel, nk=nk, has_bias=bias is not None, pre_silu=pre_silu),
        grid=(m // tm, n // tn, nk),
        in_specs=in_specs,
        out_specs=pl.BlockSpec((tm, tn), lambda i, j, k: (i, j)),
        out_shape=jax.ShapeDtypeStruct((m, n), out_dtype),
        scratch_shapes=[pltpu.VMEM((tm, tn), F32)] if nk > 1 else [],
        compiler_params=_params("parallel", "parallel", "arbitrary"),
    )(*args)


def _mmw_kernel(a_ref, w_ref, o_ref, wb_scr):
    @pl.when(pl.program_id(1) == 0)
    def _():
        wb_scr[...] = w_ref[...].astype(BF16)

    o_ref[...] = jnp.dot(a_ref[...], wb_scr[...], preferred_element_type=F32).astype(o_ref.dtype)


def matmul_w(a, w, layer, col0, n, *, out_dtype=F32, tm=1024, tn=512):
    m, kd = a.shape
    tm, tn = min(tm, m), min(tn, n)
    assert m % tm == 0 and n % tn == 0 and col0 % tn == 0 and w.shape[1] == kd
    cb = col0 // tn
    return pl.pallas_call(
        _mmw_kernel, grid=(n // tn, m // tm),
        in_specs=[pl.BlockSpec((tm, kd), lambda j, i: (i, 0)),
                  pl.BlockSpec((None, kd, tn), lambda j, i: (layer, 0, cb + j))],
        out_specs=pl.BlockSpec((tm, tn), lambda j, i: (i, j)),
        out_shape=jax.ShapeDtypeStruct((m, n), out_dtype),
        scratch_shapes=[pltpu.VMEM((kd, tn), BF16)],
        compiler_params=_params("parallel", "arbitrary"),
    )(a, w)


def _rms(x, gain):
    return x * lax.rsqrt(jnp.mean(x * x, axis=-1, keepdims=True) + NORM_EPS) * gain


def _rms_mod_kernel(h_ref, g_ref, sh_ref, sc_ref, o_ref):
    y = _rms(h_ref[...], g_ref[...])
    o_ref[...] = (y * (1.0 + sc_ref[...]) + sh_ref[...]).astype(o_ref.dtype)


def rms_mod(h, gain, shift, scale, *, tm=256):
    m, d = h.shape
    tm = min(tm, m)
    vec = pl.BlockSpec((1, d), lambda i: (0, 0))
    return pl.pallas_call(
        _rms_mod_kernel, grid=(m // tm,),
        in_specs=[pl.BlockSpec((tm, d), lambda i: (i, 0)), vec, vec, vec],
        out_specs=pl.BlockSpec((tm, d), lambda i: (i, 0)),
        out_shape=jax.ShapeDtypeStruct((m, d), BF16),
        compiler_params=_params("parallel"),
    )(h, _row(gain), _row(shift), _row(scale))


def _resid_kernel(*refs, with_next):
    h_ref, y_ref, gp_ref, gate_ref = refs[:4]
    h_new = h_ref[...] + gate_ref[...] * _rms(y_ref[...], gp_ref[...])
    if with_next:
        gn_ref, sh_ref, sc_ref, o_ref, u_ref = refs[4:]
        u = _rms(h_new, gn_ref[...])
        u_ref[...] = (u * (1.0 + sc_ref[...]) + sh_ref[...]).astype(u_ref.dtype)
    else:
        o_ref = refs[4]
    o_ref[...] = h_new


def resid_norm(h, y, gain_post, gate, nxt=None, *, tm=256):
    m, d = h.shape
    tm = min(tm, m)
    vec = pl.BlockSpec((1, d), lambda i: (0, 0))
    blk = pl.BlockSpec((tm, d), lambda i: (i, 0))
    args = [h, y, _row(gain_post), _row(gate)]
    in_specs = [blk, blk, vec, vec]
    out_specs, out_shape = blk, jax.ShapeDtypeStruct((m, d), F32)
    if nxt is not None:
        args += [_row(v) for v in nxt]
        in_specs += [vec, vec, vec]
        out_specs = (blk, blk)
        out_shape = (out_shape, jax.ShapeDtypeStruct((m, d), BF16))
    return pl.pallas_call(
        functools.partial(_resid_kernel, with_next=nxt is not None), grid=(m // tm,),
        in_specs=in_specs, out_specs=out_specs, out_shape=out_shape,
        compiler_params=_params("parallel"),
    )(*args)


def rope_tables(n_rows, dim):
    row = jnp.repeat(jnp.arange(n_rows, dtype=F32), GRID_W)
    col = jnp.tile(jnp.arange(GRID_W, dtype=F32), n_rows)
    quarter = dim // 4
    inv_freq = ROPE_THETA ** (-jnp.arange(quarter, dtype=F32) / quarter)
    ang_r = row[:, None] * inv_freq[None, :]
    ang_c = col[:, None] * inv_freq[None, :]
    ang = jnp.concatenate([ang_r, ang_r, ang_c, ang_c], axis=-1)
    cos, sin = jnp.cos(ang), jnp.sin(ang)
    first = (jnp.arange(dim) % (2 * quarter)) < quarter
    sin_p = jnp.where(first[None, :], -sin, 0.0)
    sin_m = jnp.where(first[None, :], 0.0, sin)
    rep = LANES // dim
    return tuple(jnp.tile(t, (1, rep)) for t in (cos, sin_p, sin_m))


def _prep_kernel(*refs, n_groups, has_gain, has_rope, quarter, scale):
    refs = list(refs)
    z_ref = refs.pop(0)
    gain_ref = refs.pop(0) if has_gain else None
    if has_rope:
        cos = refs.pop(0)[...]
        sin_p = refs.pop(0)[...]
        sin_m = refs.pop(0)[...]
    o_ref = refs.pop(0)
    for j in range(n_groups):
        x = z_ref[:, j * LANES:(j + 1) * LANES]
        if has_gain:
            x = _rms(x, gain_ref[...])
        if has_rope:
            x = (x * cos + pltpu.roll(x, LANES - quarter, 1) * sin_p
                 + pltpu.roll(x, quarter, 1) * sin_m)
        if scale != 1.0:
            x = x * scale
        o_ref[:, j * LANES:(j + 1) * LANES] = x.astype(o_ref.dtype)


def prep(z, col_block, width, *, gain=None, rope=None, quarter=0, scale=1.0, tm=256):
    m = z.shape[0]
    tm = min(tm, m)
    args = [z]
    in_specs = [pl.BlockSpec((tm, width), lambda i: (i, col_block))]
    if gain is not None:
        args.append(_row(gain))
        in_specs.append(pl.BlockSpec((1, LANES), lambda i: (0, 0)))
    if rope is not None:
        args += list(rope)
        in_specs += [pl.BlockSpec((tm, LANES), lambda i: (i, 0))] * 3
    return pl.pallas_call(
        functools.partial(_prep_kernel, n_groups=width // LANES, has_gain=gain is not None,
                          has_rope=rope is not None, quarter=quarter, scale=scale),
        grid=(m // tm,), in_specs=in_specs,
        out_specs=pl.BlockSpec((tm, width), lambda i: (i, 0)),
        out_shape=jax.ShapeDtypeStruct((m, width), BF16),
        compiler_params=_params("parallel"),
    )(*args)


def _flash_kernel(*refs, mode, g, tq, tk, n_chunks, has_sink, lam_init):
    refs = list(refs)
    q_ref, k_ref, v_ref, kc_ref, vc_ref = refs[:5]
    rest = refs[5:]
    if mode == "diff":
        lam_ref, subln_ref, o_ref, m_scr, l_scr, acc_scr, sa_scr, sb_scr = rest
    elif has_sink:
        sink_ref, o_ref, m_scr, l_scr, acc_scr, sa_scr, sb_scr = rest
    else:
        o_ref, m_scr, l_scr, acc_scr, sa_scr, sb_scr = rest

    if mode == "diff":
        q = q_ref[...]
        lane = lax.broadcasted_iota(jnp.int32, q.shape, 1)
        zero = jnp.zeros_like(q)
        qs = jnp.concatenate([jnp.where(lane < DA_SUB, q, zero),
                              jnp.where(lane >= DA_SUB, q, zero)], axis=0)
    else:
        qs = jnp.concatenate([q_ref[:, h * LANES:(h + 1) * LANES] for h in range(g)], axis=0)

    sink_col = None
    if has_sink:
        kv = pl.program_id(0)
        sink_col = jnp.concatenate(
            [jnp.full((tq, 1), sink_ref[kv * g + h] * LOG2E, F32) for h in range(g)], axis=0)

    def with_ones(vals):
        return jnp.concatenate([vals, jnp.ones(vals.shape, BF16)], axis=1)

    s = lax.dot_general(qs, kc_ref[...], _NT, preferred_element_type=F32)
    m = jnp.max(s, axis=-1, keepdims=True)
    if has_sink:
        m = jnp.maximum(m, sink_col)
    pv = jnp.dot(jnp.exp2(s - m).astype(BF16), with_ones(vc_ref[...]), preferred_element_type=F32)
    acc, l_rep = pv[:, :LANES], pv[:, LANES:]

    if n_chunks > 0:
        m_scr[...] = jnp.broadcast_to(m, (g * tq, LANES))
        l_scr[...] = l_rep
        acc_scr[...] = acc

        def scores(c):
            off = pl.multiple_of(c * tk, tk)
            return lax.dot_general(qs, k_ref[pl.ds(off, tk), :], _NT, preferred_element_type=F32)

        def update(s_scr, c):
            off = pl.multiple_of(c * tk, tk)
            vaug = with_ones(v_ref[pl.ds(off, tk), :])
            half = g * tq // 2
            for rows_ in (slice(0, half), slice(half, 2 * half)):
                sb = s_scr[rows_, :]
                m_prev = m_scr[rows_, :]
                m_new = jnp.maximum(m_prev, jnp.max(sb, axis=-1, keepdims=True))
                alpha = jnp.exp2(m_prev - m_new)
                m_rep = jnp.concatenate([m_new] * (tk // LANES), axis=1)
                pv = jnp.dot(jnp.exp2(sb - m_rep).astype(BF16), vaug, preferred_element_type=F32)
                l_scr[rows_, :] = alpha * l_scr[rows_, :] + pv[:, LANES:]
                acc_scr[rows_, :] = alpha * acc_scr[rows_, :] + pv[:, :LANES]
                m_scr[rows_, :] = m_new

        sa_scr[...] = scores(0)

        def body(j, carry):
            c0 = 2 * j
            sb_scr[...] = scores(c0 + 1)
            update(sa_scr, c0)
            sa_scr[...] = scores(jnp.minimum(c0 + 2, n_chunks - 1))
            update(sb_scr, c0 + 1)
            return carry

        lax.fori_loop(0, n_chunks // 2, body, 0)
        m, l_rep, acc = m_scr[:, 0:1], l_scr[...], acc_scr[...]

    l = l_rep[:, 0:1]
    if has_sink:
        l = l + jnp.exp2(sink_col - m)
    out = acc / l
    if mode == "diff":
        lv = lam_ref[...]
        lam = (jnp.exp(jnp.sum(lv[0:1] * lv[1:2], axis=-1, keepdims=True))
               - jnp.exp(jnp.sum(lv[2:3] * lv[3:4], axis=-1, keepdims=True)) + lam_init)
        y = out[:tq] - lam * out[tq:]
        o_ref[...] = (_rms(y, subln_ref[...]) * (1.0 - lam_init)).astype(o_ref.dtype)
    else:
        for h in range(g):
            o_ref[:, h * LANES:(h + 1) * LANES] = out[h * tq:(h + 1) * tq].astype(o_ref.dtype)


def flash(q, k, v, kc, vc, *, mode, tq, tk=1024, sink=None, lam=None, subln=None, lam_init=0.0):
    sq = q.shape[0]
    n_ctx = kc.shape[0]
    if k is None:
        k, v, n_chunks, t_main = kc, vc, 0, n_ctx
    else:
        t_main = k.shape[0]
        tk = min(tk, t_main // 2)
        n_chunks = t_main // tk
        assert n_chunks % 2 == 0 and n_chunks * tk == t_main
    tq = min(tq, sq)
    if mode == "diff":
        n_outer, g, qw = DA_HEADS, 2, LANES
    else:
        n_outer, g = k.shape[1] // LANES, q.shape[1] // k.shape[1]
        qw = g * LANES
    rows = g * tq
    kv_spec = pl.BlockSpec((t_main, LANES), lambda h, i: (0, h))
    ctx_spec = pl.BlockSpec((n_ctx, LANES), lambda h, i: (0, h))
    in_specs = [pl.BlockSpec((tq, qw), lambda h, i: (i, h)), kv_spec, kv_spec, ctx_spec, ctx_spec]
    args = [q, k, v, kc, vc]
    if mode == "diff":
        in_specs += [pl.BlockSpec(lam.shape, lambda h, i: (0, 0)),
                     pl.BlockSpec((1, LANES), lambda h, i: (0, 0))]
        args += [lam, _row(subln)]
    elif sink is not None:
        in_specs.append(pl.BlockSpec(memory_space=pltpu.SMEM))
        args.append(sink)
    return pl.pallas_call(
        functools.partial(_flash_kernel, mode=mode, g=g, tq=tq, tk=tk, n_chunks=n_chunks,
                          has_sink=sink is not None, lam_init=lam_init),
        grid=(n_outer, sq // tq), in_specs=in_specs,
        out_specs=pl.BlockSpec((tq, qw), lambda h, i: (i, h)),
        out_shape=jax.ShapeDtypeStruct((sq, n_outer * qw), BF16),
        scratch_shapes=[pltpu.VMEM((rows, LANES), F32), pltpu.VMEM((rows, LANES), F32),
                        pltpu.VMEM((rows, LANES), F32),
                        pltpu.VMEM((rows, tk), F32), pltpu.VMEM((rows, tk), F32)],
        compiler_params=_params("parallel", "arbitrary"),
    )(*args)


def _swa_kernel(q_ref, kp_ref, kn_ref, kx_ref, vp_ref, vn_ref, vx_ref, kc_ref, vc_ref, sink_ref,
                o_ref, *, g, seq):
    kv = pl.program_id(0)
    n = pl.program_id(1)
    qs = jnp.concatenate([q_ref[:, h * LANES:(h + 1) * LANES] for h in range(g)], axis=0)
    kw = jnp.concatenate([kp_ref[...], kn_ref[...], kx_ref[...]], axis=0)
    vw = jnp.concatenate([vp_ref[...], vn_ref[...], vx_ref[...]], axis=0)
    s_win = lax.dot_general(qs, kw, _NT, preferred_element_type=F32)
    rows = g * BLOCK
    start = n * BLOCK
    qpos = start + (lax.broadcasted_iota(jnp.int32, (rows, 3 * BLOCK), 0) & (BLOCK - 1))
    kpos = start - BLOCK + lax.broadcasted_iota(jnp.int32, (rows, 3 * BLOCK), 1)
    ok = (jnp.abs(qpos - kpos) <= WINDOW) & (kpos >= 0) & (kpos < seq)
    s_win = jnp.where(ok, s_win, NEG_BIG)
    s_ctx = lax.dot_general(qs, kc_ref[...], _NT, preferred_element_type=F32)
    sink_col = jnp.concatenate(
        [jnp.full((BLOCK, 1), sink_ref[kv * g + h] * LOG2E, F32) for h in range(g)], axis=0)
    m = jnp.maximum(jnp.maximum(jnp.max(s_win, axis=-1, keepdims=True),
                                jnp.max(s_ctx, axis=-1, keepdims=True)), sink_col)
    p_win = jnp.exp2(s_win - m)
    p_ctx = jnp.exp2(s_ctx - m)
    l = (jnp.sum(p_win, axis=-1, keepdims=True) + jnp.sum(p_ctx, axis=-1, keepdims=True)
         + jnp.exp2(sink_col - m))
    out = (jnp.dot(p_win.astype(BF16), vw, preferred_element_type=F32)
           + jnp.dot(p_ctx.astype(BF16), vc_ref[...], preferred_element_type=F32)) / l
    for h in range(g):
        o_ref[:, h * LANES:(h + 1) * LANES] = out[h * BLOCK:(h + 1) * BLOCK].astype(o_ref.dtype)


def swa(q, k, v, kc, vc, sink):
    seq = q.shape[0]
    nb = seq // BLOCK
    n_kv = k.shape[1] // LANES
    g = q.shape[1] // k.shape[1]
    n_ctx = kc.shape[0]
    prev = pl.BlockSpec((BLOCK, LANES), lambda h, i: (jnp.maximum(i - 1, 0), h))
    cur = pl.BlockSpec((BLOCK, LANES), lambda h, i: (i, h))
    nxt = pl.BlockSpec((BLOCK, LANES), lambda h, i: (jnp.minimum(i + 1, nb - 1), h))
    ctx_spec = pl.BlockSpec((n_ctx, LANES), lambda h, i: (0, h))
    qspec = pl.BlockSpec((BLOCK, g * LANES), lambda h, i: (i, h))
    return pl.pallas_call(
        functools.partial(_swa_kernel, g=g, seq=seq), grid=(n_kv, nb),
        in_specs=[qspec, prev, cur, nxt, prev, cur, nxt, ctx_spec, ctx_spec,
                  pl.BlockSpec(memory_space=pltpu.SMEM)],
        out_specs=qspec,
        out_shape=jax.ShapeDtypeStruct(q.shape, BF16),
        compiler_params=_params("parallel", "parallel"),
    )(q, k, k, k, v, v, v, kc, vc, sink)


def _head_ones():
    r = lax.broadcasted_iota(jnp.int32, (LANES, LANES), 0) >> _RW_N_BITS
    c = lax.broadcasted_iota(jnp.int32, (LANES, LANES), 1) >> _RW_N_BITS
    ones = (r == c).astype(BF16)
    return jnp.concatenate([ones, ones, ones], axis=0)


def _head_sum(x, ones3):
    hi = x.astype(BF16)
    rest = x - hi.astype(F32)
    mid = rest.astype(BF16)
    lo = (rest - mid.astype(F32)).astype(BF16)
    parts = []
    for j in range(x.shape[1] // LANES):
        sl = slice(j * LANES, (j + 1) * LANES)
        parts.append(jnp.dot(jnp.concatenate([hi[:, sl], mid[:, sl], lo[:, sl]], axis=1), ones3,
                             preferred_element_type=F32))
    return jnp.concatenate(parts, axis=1)


def _shifted(z, prev_row, next_row):
    tm = z.shape[0]
    rid = lax.broadcasted_iota(jnp.int32, (SUBLANES, z.shape[1]), 0)
    zm, zp = pltpu.roll(z, 1, 0), pltpu.roll(z, tm - 1, 0)
    zm = jnp.concatenate([jnp.where(rid == 0, prev_row, zm[:SUBLANES]), zm[SUBLANES:]], axis=0)
    zp = jnp.concatenate([zp[:tm - SUBLANES],
                          jnp.where(rid == SUBLANES - 1, next_row, zp[tm - SUBLANES:])], axis=0)
    return zm, zp


def _halo_rows(zp_ref, zn_ref, n_blocks):
    i = pl.program_id(0)
    prev_row = jnp.where(i > 0, zp_ref[SUBLANES - 1:SUBLANES, :], 0.0)
    next_row = jnp.where(i < n_blocks - 1, zn_ref[0:1, :], 0.0)
    return prev_row, next_row


def _rw_prep_kernel(z_ref, zp_ref, zn_ref, sw_ref, kk_ref, w0_ref, wup_ref, a0_ref, aup_ref,
                    gup_ref, r_o, k_o, v_o, kk_o, lwf_o, lwb_o, iclf_o, iclb_o, gate_o, *, n_blocks):
    z = z_ref[...]
    prev_row, next_row = _halo_rows(zp_ref, zn_ref, n_blocks)
    zm, zp = _shifted(z, prev_row, next_row)
    t = zm * sw_ref[0:1, :] + z * sw_ref[1:2, :] + zp * sw_ref[2:3, :]
    r_o[...] = t[:, 0:RW_W]
    k = t[:, RW_W:2 * RW_W]
    k_o[...] = k
    v_o[...] = t[:, 2 * RW_W:3 * RW_W]
    kk = k * kk_ref[...]
    ss = _head_sum(kk * kk, _head_ones())
    kk_o[...] = kk * lax.rsqrt(jnp.maximum(ss, 1e-24))
    base = 3 * RW_W
    for d, (lw_o, icl_o) in enumerate(((lwf_o, iclf_o), (lwb_o, iclb_o))):
        xw = t[:, base + d * RW_DECAY_R: base + (d + 1) * RW_DECAY_R]
        pre = w0_ref[d:d + 1, :] + jnp.dot(jnp.tanh(xw).astype(BF16), wup_ref[d],
                                           preferred_element_type=F32)
        lw_o[...] = -RW_DECAY_SCALE * jax.nn.sigmoid(pre)
        a_base = base + 2 * RW_DECAY_R
        xa = t[:, a_base + d * RW_A_R: a_base + (d + 1) * RW_A_R]
        icl_o[...] = jax.nn.sigmoid(a0_ref[d:d + 1, :] + jnp.dot(
            xa.astype(BF16), aup_ref[d], preferred_element_type=F32))
    xg = t[:, base + 2 * RW_DECAY_R + 2 * RW_A_R:]
    gate_o[...] = jnp.dot(jax.nn.sigmoid(xg).astype(BF16), gup_ref[...],
                          preferred_element_type=F32)


def rw_prep(zr, shift_w, k_k, w0, w_up, a0, a_up, g_up, *, tm=128):
    m, w = zr.shape
    tm = min(tm, m)
    nb = m // tm
    per8 = tm // SUBLANES
    last8 = m // SUBLANES - 1
    full = lambda a: pl.BlockSpec(a.shape, lambda i: (0,) * a.ndim)
    out_blk = pl.BlockSpec((tm, RW_W), lambda i: (i, 0))
    args = [zr, zr, zr, shift_w, _row(k_k), w0, w_up, a0, a_up, g_up]
    in_specs = [pl.BlockSpec((tm, w), lambda i: (i, 0)),
                pl.BlockSpec((SUBLANES, w), lambda i: (jnp.maximum(i * per8 - 1, 0), 0)),
                pl.BlockSpec((SUBLANES, w), lambda i: (jnp.minimum((i + 1) * per8, last8), 0)),
                ] + [full(a) for a in args[3:]]
    return pl.pallas_call(
        functools.partial(_rw_prep_kernel, n_blocks=nb), grid=(nb,),
        in_specs=in_specs, out_specs=(out_blk,) * 9,
        out_shape=(jax.ShapeDtypeStruct((m, RW_W), F32),) * 9,
        compiler_params=_params("parallel"),
    )(*args)


_MM = (((1,), (0,)), ((), ()))


def _dot1(a, b, dims=_MM):
    return lax.dot_general(a.astype(BF16), b.astype(BF16), dims, preferred_element_type=F32)


def _hilo(x):
    hi = x.astype(BF16)
    return hi, (x - hi.astype(F32)).astype(BF16)


def _dot2(a, b):
    ah = a.astype(BF16)
    bh, bl = _hilo(b)
    return jnp.dot(jnp.concatenate([ah, ah], axis=1), jnp.concatenate([bh, bl], axis=0),
                   preferred_element_type=F32)


def _scan_chunk(r, k, v, kk, lw, icl, ka, states, *, reverse):
    C = RW_CHUNK
    n_pairs = len(states)
    each = range(n_pairs)
    kd = k * (1.0 + (icl - 1.0) * ka)
    a = -kk
    b = kk * icl

    ti = lax.broadcasted_iota(jnp.int32, (C, C), 0)
    si = lax.broadcasted_iota(jnp.int32, (C, C), 1)
    before = (si >= ti) if reverse else (si <= ti)
    cl = jnp.dot(before.astype(F32), lw, precision=HI, preferred_element_type=F32)
    tot = cl[0:1, :] if reverse else cl[C - 1:C, :]
    e_neg = jnp.exp(-cl)
    rt = r * jnp.exp(cl)
    at = a * jnp.exp(cl - lw)
    bt = b * e_neg
    kt = kd * e_neg

    lane = lax.broadcasted_iota(jnp.int32, (C, LANES), 1)
    h0 = lane < RW_N
    zero = jnp.zeros((C, LANES), F32)

    def stack(x):
        return jnp.concatenate([jnp.where(h0, x, zero), jnp.where(h0, zero, x)], axis=0)

    def pair(x, p):
        return x[:, p * LANES:(p + 1) * LANES]

    la = [stack(pair(at, p)) for p in each]
    lr = [stack(pair(rt, p)) for p in each]
    vv = [stack(pair(v, p)) for p in each]
    sc = [_dot1(jnp.concatenate([la[p], lr[p]], axis=0),
                jnp.concatenate([pair(bt, p), pair(bt, p), pair(kt, p), pair(kt, p)], axis=0), _NT)
          for p in each]

    row = lax.broadcasted_iota(jnp.int32, (2 * C, 2 * C), 0)
    col = lax.broadcasted_iota(jnp.int32, (2 * C, 2 * C), 1)
    same_head = (row >> _RW_N_BITS) == (col >> _RW_N_BITS)
    strict = same_head & ((col > row) if reverse else (col < row))
    incl = same_head & ((col >= row) if reverse else (col <= row))
    x_ab = [jnp.where(strict, sc[p][0:2 * C, 0:2 * C], 0.0) for p in each]
    x_ak = [jnp.where(strict, sc[p][0:2 * C, 2 * C:4 * C], 0.0) for p in each]
    x_rb = [jnp.where(incl, sc[p][2 * C:4 * C, 0:2 * C], 0.0) for p in each]
    x_rk = [jnp.where(incl, sc[p][2 * C:4 * C, 2 * C:4 * C], 0.0) for p in each]

    def off_mask(bsz):
        bits = (2 * bsz).bit_length() - 1
        blk = (row >> bits) == (col >> bits)
        lo, hi = (row & (2 * bsz - 1)) < bsz, (col & (2 * bsz - 1)) < bsz
        return blk & ((lo & ~hi) if reverse else (~lo & hi))

    eye = (row == col).astype(F32)
    mask1 = off_mask(1)
    tinv = [eye + jnp.where(mask1, x_ab[p], 0.0) for p in each]
    bsz = 2
    while bsz < C:
        mask = off_mask(bsz)
        half = [_dot2(tinv[p], jnp.where(mask, x_ab[p], 0.0)) for p in each]
        tinv = [tinv[p] + _dot2(half[p], tinv[p]) for p in each]
        bsz *= 2

    z = [_dot1(la[p], states[p], _NT) + _dot1(x_ak[p], vv[p]) for p in each]
    u = [_dot2(tinv[p], z[p]) for p in each]
    y = [_dot1(lr[p], states[p], _NT)
         + _dot1(jnp.concatenate([x_rb[p], x_rk[p]], axis=1), jnp.concatenate([u[p], vv[p]], axis=0))
         for p in each]
    upd = [_dot1(jnp.concatenate([u[p][0:C] + u[p][C:2 * C], pair(v, p)], axis=0),
                 jnp.concatenate([pair(bt, p), pair(kt, p)], axis=0), _TN) for p in each]
    decay = jnp.exp(tot)
    new_states = [(states[p] + jnp.where(same_head, upd[p], 0.0)) * pair(decay, p) for p in each]
    ys = [y[p][0:C] + y[p][C:2 * C] for p in each]
    return ys, new_states


def _scan_kernel(r_ref, k_ref, v_ref, kk_ref, lw_ref, icl_ref, ka_ref, s0_ref, y_ref, send_ref,
                 s_scr, *, reverse, n_chunks, pairs):
    c = pl.program_id(1)

    @pl.when(c == 0)
    def _():
        s_scr[...] = s0_ref[...]

    ys, new_states = _scan_chunk(r_ref[...], k_ref[...], v_ref[...], kk_ref[...], lw_ref[...],
                                 icl_ref[...], ka_ref[...], [s_scr[p] for p in range(pairs)],
                                 reverse=reverse)
    for p in range(pairs):
        y_ref[:, p * LANES:(p + 1) * LANES] = ys[p]
        s_scr[p] = new_states[p]

    @pl.when(c == n_chunks - 1)
    def _():
        send_ref[...] = s_scr[...]


def rw_scan(r, k, v, kk, lw, icl, k_a, s0, *, reverse, pairs=8):
    t = r.shape[0]
    n_chunks = t // RW_CHUNK
    n_pairs = RW_W // LANES
    w = pairs * LANES
    if reverse:
        blk = pl.BlockSpec((RW_CHUNK, w), lambda p, c: (n_chunks - 1 - c, p))
    else:
        blk = pl.BlockSpec((RW_CHUNK, w), lambda p, c: (c, p))
    st = pl.BlockSpec((pairs, LANES, LANES), lambda p, c: (p, 0, 0))
    return pl.pallas_call(
        functools.partial(_scan_kernel, reverse=reverse, n_chunks=n_chunks, pairs=pairs),
        grid=(n_pairs // pairs, n_chunks),
        in_specs=[blk] * 6 + [pl.BlockSpec((1, w), lambda p, c: (0, p)), st],
        out_specs=(blk, st),
        out_shape=(jax.ShapeDtypeStruct((t, RW_W), F32),
                   jax.ShapeDtypeStruct((n_pairs, LANES, LANES), F32)),
        scratch_shapes=[pltpu.VMEM((pairs, LANES, LANES), F32)],
        compiler_params=_params("parallel", "arbitrary"),
    )(r, k, v, kk, lw, icl, _row(k_a), s0)


def _rw_post_kernel(yf_ref, yb_ref, r_ref, k_ref, v_ref, iclf_ref, iclb_ref, gate_ref,
                    ka_ref, rk_ref, lnw_ref, lnb_ref, o_ref):
    ones = _head_ones()
    y = yf_ref[...] + yb_ref[...]
    mu = _head_sum(y, ones) * (1.0 / RW_N)
    yc = y - mu
    var = _head_sum(yc * yc, ones) * (1.0 / RW_N)
    yn = yc * lax.rsqrt(var + RW_LN_EPS) * lnw_ref[...] + lnb_ref[...]
    k = k_ref[...]
    ka = ka_ref[...]
    rrk = r_ref[...] * rk_ref[...]
    kd_f = k * (1.0 + (iclf_ref[...] - 1.0) * ka)
    kd_b = k * (1.0 + (iclb_ref[...] - 1.0) * ka)
    bonus = (_head_sum(rrk * kd_f, ones) + _head_sum(rrk * kd_b, ones)) * v_ref[...]
    o_ref[...] = ((yn + bonus) * gate_ref[...]).astype(o_ref.dtype)


def rw_post(yf, yb, r, k, v, icl_f, icl_b, gate, k_a, r_k, ln_w, ln_b, *, tm=256):
    m = yf.shape[0]
    tm = min(tm, m)
    blk = pl.BlockSpec((tm, RW_W), lambda i: (i, 0))
    vec = pl.BlockSpec((1, RW_W), lambda i: (0, 0))
    return pl.pallas_call(
        _rw_post_kernel, grid=(m // tm,),
        in_specs=[blk] * 8 + [vec] * 4, out_specs=blk,
        out_shape=jax.ShapeDtypeStruct((m, RW_W), BF16),
        compiler_params=_params("parallel"),
    )(yf, yb, r, k, v, icl_f, icl_b, gate, _row(k_a), _row(r_k.reshape(-1)), _row(ln_w), _row(ln_b))


def _merge_kernel(ya_ref, yb_ref, yr_ref, yd_ref, zg_ref, bu_ref, gu_ref, gb_ref, o_ref):
    zg = zg_ref[...].astype(BF16)
    acc = None
    for bi, y_ref in enumerate((ya_ref, yb_ref, yr_ref, yd_ref)):
        gate = jax.nn.sigmoid(jnp.dot(zg, gu_ref[bi], preferred_element_type=F32) + gb_ref[bi])
        term = gate * jnp.dot(y_ref[...], bu_ref[bi], preferred_element_type=F32)
        acc = term if acc is None else acc + term
    o_ref[...] = acc.astype(o_ref.dtype)


def merge(ys, zg, branch_up, gate_up, gate_bias, *, tm=1024, tn=512):
    m = zg.shape[0]
    d = branch_up.shape[-1]
    tm, tn = min(tm, m), min(tn, d)
    yblk = pl.BlockSpec((tm, BRANCH_W), lambda i, j: (i, 0))
    return pl.pallas_call(
        _merge_kernel, grid=(m // tm, d // tn),
        in_specs=[yblk] * 4 + [pl.BlockSpec((tm, GATE_R), lambda i, j: (i, 0)),
                               pl.BlockSpec((N_BRANCH, BRANCH_W, tn), lambda i, j: (0, 0, j)),
                               pl.BlockSpec((N_BRANCH, GATE_R, tn), lambda i, j: (0, 0, j)),
                               pl.BlockSpec((N_BRANCH, 1, tn), lambda i, j: (0, 0, j))],
        out_specs=pl.BlockSpec((tm, tn), lambda i, j: (i, j)),
        out_shape=jax.ShapeDtypeStruct((m, d), BF16),
        compiler_params=_params("parallel", "parallel"),
    )(*ys, zg, branch_up, gate_up, gate_bias.reshape(N_BRANCH, 1, d))


def _conv_act_kernel(g_ref, gp_ref, gn_ref, x_ref, xp_ref, xn_ref, wg_ref, wx_ref, o_ref, *, n_blocks):
    def conv(z_ref, zp_ref, zn_ref, w_ref):
        z = z_ref[...]
        prev_row, next_row = _halo_rows(zp_ref, zn_ref, n_blocks)
        zm, zp = _shifted(z, prev_row, next_row)
        return zm * w_ref[0:1, :] + z * w_ref[1:2, :] + zp * w_ref[2:3, :]
    gate = conv(g_ref, gp_ref, gn_ref, wg_ref)
    val = conv(x_ref, xp_ref, xn_ref, wx_ref)
    o_ref[...] = (gate * jax.nn.sigmoid(gate) * val).astype(o_ref.dtype)


def conv_act(hid, w_conv, *, tm=256, tn=1024):
    m, two_f = hid.shape
    f = two_f // 2
    tm, tn = min(tm, m), min(tn, f)
    nb, nj = m // tm, f // tn
    per8 = tm // SUBLANES
    last8 = m // SUBLANES - 1

    def specs(off):
        return [pl.BlockSpec((tm, tn), lambda i, j: (i, j + off)),
                pl.BlockSpec((SUBLANES, tn), lambda i, j: (jnp.maximum(i * per8 - 1, 0), j + off)),
                pl.BlockSpec((SUBLANES, tn), lambda i, j: (jnp.minimum((i + 1) * per8, last8), j + off))]
    wspec = lambda off: pl.BlockSpec((3, tn), lambda i, j: (0, j + off))
    return pl.pallas_call(
        functools.partial(_conv_act_kernel, n_blocks=nb), grid=(nb, nj),
        in_specs=specs(0) + specs(nj) + [wspec(0), wspec(nj)],
        out_specs=pl.BlockSpec((tm, tn), lambda i, j: (i, j)),
        out_shape=jax.ShapeDtypeStruct((m, f), BF16),
        compiler_params=_params("parallel", "parallel"),
    )(hid, hid, hid, hid, hid, hid, w_conv, w_conv)


def diff_mixer(za, zac, lam_vec, subln, lam_init, tabs, need_ctx):
    w = DA_HEADS * HEAD_DIM
    scale = DA_SUB ** -0.5 * LOG2E
    q = prep(za, 0, w, rope=tabs, quarter=DA_SUB // 4, scale=scale)
    k = prep(za, 1, w, rope=tabs, quarter=DA_SUB // 4)
    v = prep(za, 2, w)
    qc = prep(zac, 0, w, scale=scale)
    kc = prep(zac, 1, w)
    vc = prep(zac, 2, w)
    kw = dict(mode="diff", lam=lam_vec, subln=subln, lam_init=lam_init)
    y = flash(q, k, v, kc, vc, tq=512, **kw)
    yc = flash(qc, None, None, kc, vc, tq=256, **kw) if need_ctx else None
    return y, yc


def gqa_mixer(zb, zbc, q_gain, k_gain, tabs, need_ctx):
    wq, wk = GQA_HEADS * HEAD_DIM, GQA_KV * HEAD_DIM
    scale = HEAD_DIM ** -0.5 * LOG2E
    kcol, vcol = wq // wk, wq // wk + 1
    q = prep(zb, 0, wq, gain=q_gain, rope=tabs, quarter=HEAD_DIM // 4, scale=scale)
    k = prep(zb, kcol, wk, gain=k_gain, rope=tabs, quarter=HEAD_DIM // 4)
    v = prep(zb, vcol, wk)
    qc = prep(zbc, 0, wq, gain=q_gain, scale=scale)
    kc = prep(zbc, kcol, wk, gain=k_gain)
    vc = prep(zbc, vcol, wk)
    y = flash(q, k, v, kc, vc, mode="gqa", tq=256)
    yc = flash(qc, None, None, kc, vc, mode="gqa", tq=128) if need_ctx else None
    return y, yc


def swa_mixer(zd, zdc, sink, tabs, need_ctx):
    wq, wk = SWA_HEADS * HEAD_DIM, SWA_KV * HEAD_DIM
    scale = HEAD_DIM ** -0.5 * LOG2E
    kcol, vcol = wq // wk, wq // wk + 1
    q = prep(zd, 0, wq, rope=tabs, quarter=HEAD_DIM // 4, scale=scale)
    k = prep(zd, kcol, wk, rope=tabs, quarter=HEAD_DIM // 4)
    v = prep(zd, vcol, wk)
    qc = prep(zdc, 0, wq, scale=scale)
    kc = prep(zdc, kcol, wk)
    vc = prep(zdc, vcol, wk)
    y = swa(q, k, v, kc, vc, sink)
    yc = flash(qc, None, None, kc, vc, mode="gqa", tq=128, sink=sink) if need_ctx else None
    return y, yc


def rwkv_mixer(zr, zrc, p, need_ctx):
    prep_args = (p["shift"], p["k_k"], p["w0"], p["w_up"], p["a0"], p["a_up"], p["g_up"])
    r, k, v, kk, lwf, lwb, iclf, iclb, gate = rw_prep(zr, *prep_args)
    rc, kc, vc, kkc, lwfc, lwbc, iclfc, iclbc, gatec = rw_prep(zrc, *prep_args)
    s0 = jnp.zeros((RW_W // LANES, LANES, LANES), F32)
    ys, ycs = [], []
    for reverse, lw, icl, lwc, iclc in ((False, lwf, iclf, lwfc, iclfc), (True, lwb, iclb, lwbc, iclbc)):
        yc_d, s_ctx = rw_scan(rc, kc, vc, kkc, lwc, iclc, p["k_a"], s0, reverse=reverse)
        y_d, _ = rw_scan(r, k, v, kk, lw, icl, p["k_a"], s_ctx, reverse=reverse)
        ys.append(y_d)
        ycs.append(yc_d)
    post_args = (p["k_a"], p["r_k"], p["ln_w"], p["ln_b"])
    y = rw_post(ys[0], ys[1], r, k, v, iclf, iclb, gate, *post_args)
    yc = rw_post(ycs[0], ycs[1], rc, kc, vc, iclfc, iclbc, gatec, *post_args) if need_ctx else None
    return y, yc


_SLAB_TILES = ((1024, 512), (1024, 512), (512, 768), (512, 768), (1024, 256))


def kernel(x, c, ctx, c_ctx, mod_down, mod_up, mod_bias, norm_mix_pre, norm_mix_post, norm_ffn_pre, norm_ffn_post, w_in, diff_lambda, diff_subln, gqa_q_norm, gqa_k_norm, rwkv_shift, rwkv_w0, rwkv_w_up, rwkv_a0, rwkv_a_up, rwkv_g_up, rwkv_k_k, rwkv_k_a, rwkv_r_k, rwkv_ln_w, rwkv_ln_b, swa_sink, branch_up, gate_up, gate_bias, w_out, ffn_up, ffn_conv, ffn_down):
    depth = w_in.shape[0]
    s, d = x.shape[1], x.shape[2]
    n_rows = s // GRID_W
    tabs_h = rope_tables(n_rows, HEAD_DIM)
    tabs_s = rope_tables(n_rows, DA_SUB)
    h, hc = x[0], ctx[0]
    cvec = jnp.zeros((16, d), F32).at[0].set(c[0]).at[1].set(c_ctx)
    mod_rows = []
    for l in range(depth):
        md = matmul(cvec, mod_down[l].astype(BF16), pre_silu=True, tm=16).astype(BF16)
        mods = matmul(md, mod_up[l].astype(BF16), bias=_row(mod_bias[l]), tm=16, tn=2048)
        mod_rows.append((mods[0].reshape(N_MOD, d), mods[1].reshape(N_MOD, d)))
    u = rms_mod(h, norm_mix_pre[0], mod_rows[0][0][0], mod_rows[0][0][1])
    uc = rms_mod(hc, norm_mix_pre[0], mod_rows[0][1][0], mod_rows[0][1][1])
    for l in range(depth):
        need_ctx = l < depth - 1
        lam_init = 0.8 - 0.6 * math.exp(-0.3 * l)
        m, mc = mod_rows[l]
        rw_p = dict(shift=rwkv_shift[l], k_k=rwkv_k_k[l], w0=rwkv_w0[l], w_up=rwkv_w_up[l].astype(BF16),
                    a0=rwkv_a0[l], a_up=rwkv_a_up[l].astype(BF16), g_up=rwkv_g_up[l].astype(BF16),
                    k_a=rwkv_k_a[l], r_k=rwkv_r_k[l], ln_w=rwkv_ln_w[l], ln_b=rwkv_ln_b[l])
        bu, gu = branch_up[l].astype(BF16), gate_up[l].astype(BF16)
        fd = ffn_down[l].astype(BF16)

        def in_proj(a):
            outs, col0 = [], 0
            for width, (tm, tn) in zip(IN_SIZES, _SLAB_TILES):
                outs.append(matmul_w(a, w_in, l, col0, width, tm=tm, tn=tn))
                col0 += width
            return outs

        za, zb, zr, zd, zg = in_proj(u)
        zac, zbc, zrc, zdc, zgc = in_proj(uc)
        ya, yac = diff_mixer(za, zac, diff_lambda[l], diff_subln[l], lam_init, tabs_s, need_ctx)
        yb, ybc = gqa_mixer(zb, zbc, gqa_q_norm[l], gqa_k_norm[l], tabs_h, need_ctx)
        yr, yrc = rwkv_mixer(zr, zrc, rw_p, need_ctx)
        yd, ydc = swa_mixer(zd, zdc, swa_sink[l], tabs_h, need_ctx)

        def sublayers(hh, ys, zgate, mm, mm_next):
            acc = merge(ys, zgate, bu, gu, gate_bias[l])
            mix = matmul_w(acc, w_out, l, 0, d)
            hh, u2 = resid_norm(hh, mix, norm_mix_post[l], mm[2], (norm_ffn_pre[l], mm[3], mm[4]))
            hid = matmul_w(u2, ffn_up, l, 0, ffn_up.shape[2], tm=512, tn=1024)
            act = conv_act(hid, ffn_conv[l])
            f = matmul(act, fd, tm=512)
            if mm_next is None:
                return resid_norm(hh, f, norm_ffn_post[l], mm[5]), None
            return resid_norm(hh, f, norm_ffn_post[l], mm[5],
                              (norm_mix_pre[l + 1], mm_next[0], mm_next[1]))

        h, u = sublayers(h, (ya, yb, yr, yd), zg, m, mod_rows[l + 1][0] if need_ctx else None)
        if need_ctx:
            hc, uc = sublayers(hc, (yac, ybc, yrc, ydc), zgc, mc, mod_rows[l + 1][1])
    return h[None]
```

```python
import functools
import math

import jax
import jax.numpy as jnp
from jax import lax
from jax.experimental import pallas as pl
from jax.experimental.pallas import tpu as pltpu

F32 = jnp.float32
BF16 = jnp.bfloat16
HI = lax.Precision.HIGHEST

GRID_W = 64
BLOCK = 128
WINDOW = 128
ROPE_THETA = 10000.0
NORM_EPS = 1e-6
HEAD_DIM = 128
DA_HEADS = 8
DA_SUB = HEAD_DIM // 2
GQA_HEADS = 8
GQA_KV = 2
SWA_HEADS = 8
SWA_KV = 2
RW_HEADS = 16
RW_N = 64
RW_W = RW_HEADS * RW_N
RW_DECAY_R = 128
RW_A_R = 128
RW_GATE_R = 256
RW_LN_EPS = 64e-5
RW_DECAY_SCALE = 0.6065306597126334
N_BRANCH = 4
BRANCH_W = 1024
GATE_R = 256
N_MOD = 6
DA_COLS = 3 * DA_HEADS * HEAD_DIM
GQA_COLS = (GQA_HEADS + 2 * GQA_KV) * HEAD_DIM
RW_COLS = 3 * RW_W + 2 * RW_DECAY_R + 2 * RW_A_R + RW_GATE_R
SWA_COLS = (SWA_HEADS + 2 * SWA_KV) * HEAD_DIM
IN_SIZES = (DA_COLS, GQA_COLS, RW_COLS, SWA_COLS, GATE_R)

LANES = 128
SUBLANES = 8
VMEM_LIMIT = 56 * 1024 * 1024
RW_CHUNK = 64
_RW_N_BITS = RW_N.bit_length() - 1
assert RW_CHUNK == RW_N == 1 << _RW_N_BITS
NEG_BIG = -1e30
LOG2E = math.log2(math.e)

_NT = (((1,), (1,)), ((), ()))
_TN = (((0,), (0,)), ((), ()))


def _params(*sem):
    return pltpu.CompilerParams(dimension_semantics=sem, vmem_limit_bytes=VMEM_LIMIT)


def _row(v):
    return v.reshape(1, -1)


def _mm_kernel(*refs, nk, has_bias, pre_silu):
    a_ref, b_ref = refs[0], refs[1]
    bias_ref = refs[2] if has_bias else None
    o_ref = refs[3] if has_bias else refs[2]
    a = a_ref[...]
    if pre_silu:
        a = (a * jax.nn.sigmoid(a)).astype(BF16)
    prod = jnp.dot(a, b_ref[...], preferred_element_type=F32)

    def finish(acc):
        if has_bias:
            acc = acc + bias_ref[...]
        o_ref[...] = acc.astype(o_ref.dtype)

    if nk == 1:
        finish(prod)
        return
    acc_ref = refs[-1]
    k = pl.program_id(2)

    @pl.when(k == 0)
    def _():
        acc_ref[...] = prod

    @pl.when(k > 0)
    def _():
        acc_ref[...] += prod

    @pl.when(k == nk - 1)
    def _():
        finish(acc_ref[...])


def matmul(a, b, *, out_dtype=F32, tm=1024, tn=512, tk=None, bias=None, pre_silu=False):
    m, kd = a.shape
    n = b.shape[1]
    tm, tn = min(tm, m), min(tn, n)
    tk = kd if tk is None else min(tk, kd)
    assert m % tm == 0 and n % tn == 0 and kd % tk == 0, (a.shape, b.shape, tm, tn, tk)
    nk = kd // tk
    in_specs = [pl.BlockSpec((tm, tk), lambda i, j, k: (i, k)),
                pl.BlockSpec((tk, tn), lambda i, j, k: (k, j))]
    args = [a, b]
    if bias is not None:
        in_specs.append(pl.BlockSpec((1, tn), lambda i, j, k: (0, j)))
        args.append(bias)
    return pl.pallas_call(
        functools.partial(_mm_kernel, nk=nk, has_bias=bias is not None, pre_silu=pre_silu),
        grid=(m // tm, n // tn, nk),
        in_specs=in_specs,
        out_specs=pl.BlockSpec((tm, tn), lambda i, j, k: (i, j)),
        out_shape=jax.ShapeDtypeStruct((m, n), out_dtype),
        scratch_shapes=[pltpu.VMEM((tm, tn), F32)] if nk > 1 else [],
        compiler_params=_params("parallel", "parallel", "arbitrary"),
    )(*args)


def _mmw_kernel(a_ref, w_ref, o_ref, wb_scr):
    @pl.when(pl.program_id(1) == 0)
    def _():
        wb_scr[...] = w_ref[...].astype(BF16)

    o_ref[...] = jnp.dot(a_ref[...], wb_scr[...], preferred_element_type=F32).astype(o_ref.dtype)


def matmul_w(a, w, layer, col0, n, *, out_dtype=F32, tm=1024, tn=512):
    m, kd = a.shape
    tm, tn = min(tm, m), min(tn, n)
    assert m % tm == 0 and n % tn == 0 and col0 % tn == 0 and w.shape[1] == kd
    cb = col0 // tn
    return pl.pallas_call(
        _mmw_kernel, grid=(n // tn, m // tm),
        in_specs=[pl.BlockSpec((tm, kd), lambda j, i: (i, 0)),
                  pl.BlockSpec((None, kd, tn), lambda j, i: (layer, 0, cb + j))],
        out_specs=pl.BlockSpec((tm, tn), lambda j, i: (i, j)),
        out_shape=jax.ShapeDtypeStruct((m, n), out_dtype),
        scratch_shapes=[pltpu.VMEM((kd, tn), BF16)],
        compiler_params=_params("parallel", "arbitrary"),
    )(a, w)


def _rms(x, gain):
    return x * lax.rsqrt(jnp.mean(x * x, axis=-1, keepdims=True) + NORM_EPS) * gain


def _rms_mod_kernel(h_ref, g_ref, sh_ref, sc_ref, o_ref):
    y = _rms(h_ref[...], g_ref[...])
    o_ref[...] = (y * (1.0 + sc_ref[...]) + sh_ref[...]).astype(o_ref.dtype)


def rms_mod(h, gain, shift, scale, *, tm=256):
    m, d = h.shape
    tm = min(tm, m)
    vec = pl.BlockSpec((1, d), lambda i: (0, 0))
    return pl.pallas_call(
        _rms_mod_kernel, grid=(m // tm,),
        in_specs=[pl.BlockSpec((tm, d), lambda i: (i, 0)), vec, vec, vec],
        out_specs=pl.BlockSpec((tm, d), lambda i: (i, 0)),
        out_shape=jax.ShapeDtypeStruct((m, d), BF16),
        compiler_params=_params("parallel"),
    )(h, _row(gain), _row(shift), _row(scale))


def _resid_kernel(*refs, with_next):
    h_ref, y_ref, gp_ref, gate_ref = refs[:4]
    h_new = h_ref[...] + gate_ref[...] * _rms(y_ref[...], gp_ref[...])
    if with_next:
        gn_ref, sh_ref, sc_ref, o_ref, u_ref = refs[4:]
        u = _rms(h_new, gn_ref[...])
        u_ref[...] = (u * (1.0 + sc_ref[...]) + sh_ref[...]).astype(u_ref.dtype)
    else:
        o_ref = refs[4]
    o_ref[...] = h_new


def resid_norm(h, y, gain_post, gate, nxt=None, *, tm=256):
    m, d = h.shape
    tm = min(tm, m)
    vec = pl.BlockSpec((1, d), lambda i: (0, 0))
    blk = pl.BlockSpec((tm, d), lambda i: (i, 0))
    args = [h, y, _row(gain_post), _row(gate)]
    in_specs = [blk, blk, vec, vec]
    out_specs, out_shape = blk, jax.ShapeDtypeStruct((m, d), F32)
    if nxt is not None:
        args += [_row(v) for v in nxt]
        in_specs += [vec, vec, vec]
        out_specs = (blk, blk)
        out_shape = (out_shape, jax.ShapeDtypeStruct((m, d), BF16))
    return pl.pallas_call(
        functools.partial(_resid_kernel, with_next=nxt is not None), grid=(m // tm,),
        in_specs=in_specs, out_specs=out_specs, out_shape=out_shape,
        compiler_params=_params("parallel"),
    )(*args)


def rope_tables(n_rows, dim):
    row = jnp.repeat(jnp.arange(n_rows, dtype=F32), GRID_W)
    col = jnp.tile(jnp.arange(GRID_W, dtype=F32), n_rows)
    quarter = dim // 4
    inv_freq = ROPE_THETA ** (-jnp.arange(quarter, dtype=F32) / quarter)
    ang_r = row[:, None] * inv_freq[None, :]
    ang_c = col[:, None] * inv_freq[None, :]
    ang = jnp.concatenate([ang_r, ang_r, ang_c, ang_c], axis=-1)
    cos, sin = jnp.cos(ang), jnp.sin(ang)
    first = (jnp.arange(dim) % (2 * quarter)) < quarter
    sin_p = jnp.where(first[None, :], -sin, 0.0)
    sin_m = jnp.where(first[None, :], 0.0, sin)
    rep = LANES // dim
    return tuple(jnp.tile(t, (1, rep)) for t in (cos, sin_p, sin_m))


def _prep_kernel(*refs, n_groups, has_gain, has_rope, quarter, scale):
    refs = list(refs)
    z_ref = refs.pop(0)
    gain_ref = refs.pop(0) if has_gain else None
    if has_rope:
        cos = refs.pop(0)[...]
        sin_p = refs.pop(0)[...]
        sin_m = refs.pop(0)[...]
    o_ref = refs.pop(0)
    for j in range(n_groups):
        x = z_ref[:, j * LANES:(j + 1) * LANES]
        if has_gain:
            x = _rms(x, gain_ref[...])
        if has_rope:
            x = (x * cos + pltpu.roll(x, LANES - quarter, 1) * sin_p
                 + pltpu.roll(x, quarter, 1) * sin_m)
        if scale != 1.0:
            x = x * scale
        o_ref[:, j * LANES:(j + 1) * LANES] = x.astype(o_ref.dtype)


def prep(z, col_block, width, *, gain=None, rope=None, quarter=0, scale=1.0, tm=256):
    m = z.shape[0]
    tm = min(tm, m)
    args = [z]
    in_specs = [pl.BlockSpec((tm, width), lambda i: (i, col_block))]
    if gain is not None:
        args.append(_row(gain))
        in_specs.append(pl.BlockSpec((1, LANES), lambda i: (0, 0)))
    if rope is not None:
        args += list(rope)
        in_specs += [pl.BlockSpec((tm, LANES), lambda i: (i, 0))] * 3
    return pl.pallas_call(
        functools.partial(_prep_kernel, n_groups=width // LANES, has_gain=gain is not None,
                          has_rope=rope is not None, quarter=quarter, scale=scale),
        grid=(m // tm,), in_specs=in_specs,
        out_specs=pl.BlockSpec((tm, width), lambda i: (i, 0)),
        out_shape=jax.ShapeDtypeStruct((m, width), BF16),
        compiler_params=_params("parallel"),
    )(*args)


def _flash_kernel(*refs, mode, g, tq, tk, n_chunks, has_sink, lam_init):
    refs = list(refs)
    q_ref, k_ref, v_ref, kc_ref, vc_ref = refs[:5]
    rest = refs[5:]
    if mode == "diff":
        lam_ref, subln_ref, o_ref, m_scr, l_scr, acc_scr, sa_scr, sb_scr = rest
    elif has_sink:
        sink_ref, o_ref, m_scr, l_scr, acc_scr, sa_scr, sb_scr = rest
    else:
        o_ref, m_scr, l_scr, acc_scr, sa_scr, sb_scr = rest

    if mode == "diff":
        q = q_ref[...]
        lane = lax.broadcasted_iota(jnp.int32, q.shape, 1)
        zero = jnp.zeros_like(q)
        qs = jnp.concatenate([jnp.where(lane < DA_SUB, q, zero),
                              jnp.where(lane >= DA_SUB, q, zero)], axis=0)
    else:
        qs = jnp.concatenate([q_ref[:, h * LANES:(h + 1) * LANES] for h in range(g)], axis=0)

    sink_col = None
    if has_sink:
        kv = pl.program_id(0)
        sink_col = jnp.concatenate(
            [jnp.full((tq, 1), sink_ref[kv * g + h] * LOG2E, F32) for h in range(g)], axis=0)

    def with_ones(vals):
        return jnp.concatenate([vals, jnp.ones(vals.shape, BF16)], axis=1)

    s = lax.dot_general(qs, kc_ref[...], _NT, preferred_element_type=F32)
    m = jnp.max(s, axis=-1, keepdims=True)
    if has_sink:
        m = jnp.maximum(m, sink_col)
    pv = jnp.dot(jnp.exp2(s - m).astype(BF16), with_ones(vc_ref[...]), preferred_element_type=F32)
    acc, l_rep = pv[:, :LANES], pv[:, LANES:]

    if n_chunks > 0:
        m_scr[...] = jnp.broadcast_to(m, (g * tq, LANES))
        l_scr[...] = l_rep
        acc_scr[...] = acc

        def scores(c):
            off = pl.multiple_of(c * tk, tk)
            return lax.dot_general(qs, k_ref[pl.ds(off, tk), :], _NT, preferred_element_type=F32)

        def update(s_scr, c):
            off = pl.multiple_of(c * tk, tk)
            vaug = with_ones(v_ref[pl.ds(off, tk), :])
            half = g * tq // 2
            for rows_ in (slice(0, half), slice(half, 2 * half)):
                sb = s_scr[rows_, :]
                m_prev = m_scr[rows_, :]
                m_new = jnp.maximum(m_prev, jnp.max(sb, axis=-1, keepdims=True))
                alpha = jnp.exp2(m_prev - m_new)
                m_rep = jnp.concatenate([m_new] * (tk // LANES), axis=1)
                pv = jnp.dot(jnp.exp2(sb - m_rep).astype(BF16), vaug, preferred_element_type=F32)
                l_scr[rows_, :] = alpha * l_scr[rows_, :] + pv[:, LANES:]
                acc_scr[rows_, :] = alpha * acc_scr[rows_, :] + pv[:, :LANES]
                m_scr[rows_, :] = m_new

        sa_scr[...] = scores(0)

        def body(j, carry):
            c0 = 2 * j
            sb_scr[...] = scores(c0 + 1)
            update(sa_scr, c0)
            sa_scr[...] = scores(jnp.minimum(c0 + 2, n_chunks - 1))
            update(sb_scr, c0 + 1)
            return carry

        lax.fori_loop(0, n_chunks // 2, body, 0)
        m, l_rep, acc = m_scr[:, 0:1], l_scr[...], acc_scr[...]

    l = l_rep[:, 0:1]
    if has_sink:
        l = l + jnp.exp2(sink_col - m)
    out = acc / l
    if mode == "diff":
        lv = lam_ref[...]
        lam = (jnp.exp(jnp.sum(lv[0:1] * lv[1:2], axis=-1, keepdims=True))
               - jnp.exp(jnp.sum(lv[2:3] * lv[3:4], axis=-1, keepdims=True)) + lam_init)
        y = out[:tq] - lam * out[tq:]
        o_ref[...] = (_rms(y, subln_ref[...]) * (1.0 - lam_init)).astype(o_ref.dtype)
    else:
        for h in range(g):
            o_ref[:, h * LANES:(h + 1) * LANES] = out[h * tq:(h + 1) * tq].astype(o_ref.dtype)


def flash(q, k, v, kc, vc, *, mode, tq, tk=1024, sink=None, lam=None, subln=None, lam_init=0.0):
    sq = q.shape[0]
    n_ctx = kc.shape[0]
    if k is None:
        k, v, n_chunks, t_main = kc, vc, 0, n_ctx
    else:
        t_main = k.shape[0]
        tk = min(tk, t_main // 2)
        n_chunks = t_main // tk
        assert n_chunks % 2 == 0 and n_chunks * tk == t_main
    tq = min(tq, sq)
    if mode == "diff":
        n_outer, g, qw = DA_HEADS, 2, LANES
    else:
        n_outer, g = k.shape[1] // LANES, q.shape[1] // k.shape[1]
        qw = g * LANES
    rows = g * tq
    kv_spec = pl.BlockSpec((t_main, LANES), lambda h, i: (0, h))
    ctx_spec = pl.BlockSpec((n_ctx, LANES), lambda h, i: (0, h))
    in_specs = [pl.BlockSpec((tq, qw), lambda h, i: (i, h)), kv_spec, kv_spec, ctx_spec, ctx_spec]
    args = [q, k, v, kc, vc]
    if mode == "diff":
        in_specs += [pl.BlockSpec(lam.shape, lambda h, i: (0, 0)),
                     pl.BlockSpec((1, LANES), lambda h, i: (0, 0))]
        args += [lam, _row(subln)]
    elif sink is not None:
        in_specs.append(pl.BlockSpec(memory_space=pltpu.SMEM))
        args.append(sink)
    return pl.pallas_call(
        functools.partial(_flash_kernel, mode=mode, g=g, tq=tq, tk=tk, n_chunks=n_chunks,
                          has_sink=sink is not None, lam_init=lam_init),
        grid=(n_outer, sq // tq), in_specs=in_specs,
        out_specs=pl.BlockSpec((tq, qw), lambda h, i: (i, h)),
        out_shape=jax.ShapeDtypeStruct((sq, n_outer * qw), BF16),
        scratch_shapes=[pltpu.VMEM((rows, LANES), F32), pltpu.VMEM((rows, LANES), F32),
                        pltpu.VMEM((rows, LANES), F32),
                        pltpu.VMEM((rows, tk), F32), pltpu.VMEM((rows, tk), F32)],
        compiler_params=_params("parallel", "arbitrary"),
    )(*args)


def _swa_kernel(q_ref, kp_ref, kn_ref, kx_ref, vp_ref, vn_ref, vx_ref, kc_ref, vc_ref, sink_ref,
                o_ref, *, g, seq, per_step):
    kv = pl.program_id(0)
    step = pl.program_id(1)
    kwin = jnp.concatenate([kp_ref[...], kn_ref[...], kx_ref[...]], axis=0)
    vwin = jnp.concatenate([vp_ref[...], vn_ref[...], vx_ref[...]], axis=0)
    rows = g * BLOCK
    sink_col = jnp.concatenate(
        [jnp.full((BLOCK, 1), sink_ref[kv * g + h] * LOG2E, F32) for h in range(g)], axis=0)
    row_in_block = lax.broadcasted_iota(jnp.int32, (rows, 3 * BLOCK), 0) & (BLOCK - 1)
    col = lax.broadcasted_iota(jnp.int32, (rows, 3 * BLOCK), 1)
    in_band = jnp.abs(row_in_block + BLOCK - col) <= WINDOW
    blocks = range(per_step)
    qs = [jnp.concatenate([q_ref[b * BLOCK:(b + 1) * BLOCK, h * LANES:(h + 1) * LANES]
                           for h in range(g)], axis=0) for b in blocks]
    s_win = [lax.dot_general(qs[b], kwin[b * BLOCK:(b + 3) * BLOCK], _NT,
                             preferred_element_type=F32) for b in blocks]
    s_ctx = [lax.dot_general(qs[b], kc_ref[...], _NT, preferred_element_type=F32) for b in blocks]
    for b in blocks:
        kpos = (step * per_step + b - 1) * BLOCK + col
        s_win[b] = jnp.where(in_band & (kpos >= 0) & (kpos < seq), s_win[b], NEG_BIG)
    m = [jnp.maximum(jnp.maximum(jnp.max(s_win[b], axis=-1, keepdims=True),
                                 jnp.max(s_ctx[b], axis=-1, keepdims=True)), sink_col)
         for b in blocks]
    p_win = [jnp.exp2(s_win[b] - m[b]) for b in blocks]
    p_ctx = [jnp.exp2(s_ctx[b] - m[b]) for b in blocks]
    l = [jnp.sum(p_win[b], axis=-1, keepdims=True) + jnp.sum(p_ctx[b], axis=-1, keepdims=True)
         + jnp.exp2(sink_col - m[b]) for b in blocks]
    out = [(jnp.dot(p_win[b].astype(BF16), vwin[b * BLOCK:(b + 3) * BLOCK],
                    preferred_element_type=F32)
            + jnp.dot(p_ctx[b].astype(BF16), vc_ref[...], preferred_element_type=F32)) / l[b]
           for b in blocks]
    for b in blocks:
        for h in range(g):
            o_ref[b * BLOCK:(b + 1) * BLOCK, h * LANES:(h + 1) * LANES] = (
                out[b][h * BLOCK:(h + 1) * BLOCK].astype(o_ref.dtype))


def swa(q, k, v, kc, vc, sink, *, per_step=4):
    seq = q.shape[0]
    nb = seq // BLOCK
    per_step = math.gcd(per_step, nb)
    n_kv = k.shape[1] // LANES
    g = q.shape[1] // k.shape[1]
    n_ctx = kc.shape[0]
    prev = pl.BlockSpec((BLOCK, LANES), lambda h, i: (jnp.maximum(i * per_step - 1, 0), h))
    cur = pl.BlockSpec((per_step * BLOCK, LANES), lambda h, i: (i, h))
    nxt = pl.BlockSpec((BLOCK, LANES), lambda h, i: (jnp.minimum((i + 1) * per_step, nb - 1), h))
    ctx_spec = pl.BlockSpec((n_ctx, LANES), lambda h, i: (0, h))
    qspec = pl.BlockSpec((per_step * BLOCK, g * LANES), lambda h, i: (i, h))
    return pl.pallas_call(
        functools.partial(_swa_kernel, g=g, seq=seq, per_step=per_step), grid=(n_kv, nb // per_step),
        in_specs=[qspec, prev, cur, nxt, prev, cur, nxt, ctx_spec, ctx_spec,
                  pl.BlockSpec(memory_space=pltpu.SMEM)],
        out_specs=qspec,
        out_shape=jax.ShapeDtypeStruct(q.shape, BF16),
        compiler_params=_params("parallel", "parallel"),
    )(q, k, k, k, v, v, v, kc, vc, sink)


def _head_ones():
    r = lax.broadcasted_iota(jnp.int32, (LANES, LANES), 0) >> _RW_N_BITS
    c = lax.broadcasted_iota(jnp.int32, (LANES, LANES), 1) >> _RW_N_BITS
    ones = (r == c).astype(BF16)
    return jnp.concatenate([ones, ones, ones], axis=0)


def _head_sum(x, ones3):
    hi = x.astype(BF16)
    rest = x - hi.astype(F32)
    mid = rest.astype(BF16)
    lo = (rest - mid.astype(F32)).astype(BF16)
    parts = []
    for j in range(x.shape[1] // LANES):
        sl = slice(j * LANES, (j + 1) * LANES)
        parts.append(jnp.dot(jnp.concatenate([hi[:, sl], mid[:, sl], lo[:, sl]], axis=1), ones3,
                             preferred_element_type=F32))
    return jnp.concatenate(parts, axis=1)


def _shifted(z, prev_row, next_row):
    tm = z.shape[0]
    rid = lax.broadcasted_iota(jnp.int32, (SUBLANES, z.shape[1]), 0)
    zm, zp = pltpu.roll(z, 1, 0), pltpu.roll(z, tm - 1, 0)
    zm = jnp.concatenate([jnp.where(rid == 0, prev_row, zm[:SUBLANES]), zm[SUBLANES:]], axis=0)
    zp = jnp.concatenate([zp[:tm - SUBLANES],
                          jnp.where(rid == SUBLANES - 1, next_row, zp[tm - SUBLANES:])], axis=0)
    return zm, zp


def _halo_rows(zp_ref, zn_ref, n_blocks):
    i = pl.program_id(0)
    prev_row = jnp.where(i > 0, zp_ref[SUBLANES - 1:SUBLANES, :], 0.0)
    next_row = jnp.where(i < n_blocks - 1, zn_ref[0:1, :], 0.0)
    return prev_row, next_row


def _rw_prep_kernel(z_ref, zp_ref, zn_ref, sw_ref, kk_ref, w0_ref, wup_ref, a0_ref, aup_ref,
                    gup_ref, r_o, k_o, v_o, kk_o, lwf_o, lwb_o, iclf_o, iclb_o, gate_o, *, n_blocks):
    z = z_ref[...]
    prev_row, next_row = _halo_rows(zp_ref, zn_ref, n_blocks)
    zm, zp = _shifted(z, prev_row, next_row)
    t = zm * sw_ref[0:1, :] + z * sw_ref[1:2, :] + zp * sw_ref[2:3, :]
    r_o[...] = t[:, 0:RW_W]
    k = t[:, RW_W:2 * RW_W]
    k_o[...] = k
    v_o[...] = t[:, 2 * RW_W:3 * RW_W]
    kk = k * kk_ref[...]
    ss = _head_sum(kk * kk, _head_ones())
    kk_o[...] = kk * lax.rsqrt(jnp.maximum(ss, 1e-24))
    base = 3 * RW_W
    for d, (lw_o, icl_o) in enumerate(((lwf_o, iclf_o), (lwb_o, iclb_o))):
        xw = t[:, base + d * RW_DECAY_R: base + (d + 1) * RW_DECAY_R]
        pre = w0_ref[d:d + 1, :] + jnp.dot(jnp.tanh(xw).astype(BF16), wup_ref[d],
                                           preferred_element_type=F32)
        lw_o[...] = -RW_DECAY_SCALE * jax.nn.sigmoid(pre)
        a_base = base + 2 * RW_DECAY_R
        xa = t[:, a_base + d * RW_A_R: a_base + (d + 1) * RW_A_R]
        icl_o[...] = jax.nn.sigmoid(a0_ref[d:d + 1, :] + jnp.dot(
            xa.astype(BF16), aup_ref[d], preferred_element_type=F32))
    xg = t[:, base + 2 * RW_DECAY_R + 2 * RW_A_R:]
    gate_o[...] = jnp.dot(jax.nn.sigmoid(xg).astype(BF16), gup_ref[...],
                          preferred_element_type=F32)


def rw_prep(zr, shift_w, k_k, w0, w_up, a0, a_up, g_up, *, tm=128):
    m, w = zr.shape
    tm = min(tm, m)
    nb = m // tm
    per8 = tm // SUBLANES
    last8 = m // SUBLANES - 1
    full = lambda a: pl.BlockSpec(a.shape, lambda i: (0,) * a.ndim)
    out_blk = pl.BlockSpec((tm, RW_W), lambda i: (i, 0))
    args = [zr, zr, zr, shift_w, _row(k_k), w0, w_up, a0, a_up, g_up]
    in_specs = [pl.BlockSpec((tm, w), lambda i: (i, 0)),
                pl.BlockSpec((SUBLANES, w), lambda i: (jnp.maximum(i * per8 - 1, 0), 0)),
                pl.BlockSpec((SUBLANES, w), lambda i: (jnp.minimum((i + 1) * per8, last8), 0)),
                ] + [full(a) for a in args[3:]]
    return pl.pallas_call(
        functools.partial(_rw_prep_kernel, n_blocks=nb), grid=(nb,),
        in_specs=in_specs, out_specs=(out_blk,) * 9,
        out_shape=(jax.ShapeDtypeStruct((m, RW_W), F32),) * 9,
        compiler_params=_params("parallel"),
    )(*args)


_MM = (((1,), (0,)), ((), ()))


def _dot1(a, b, dims=_MM):
    return lax.dot_general(a.astype(BF16), b.astype(BF16), dims, preferred_element_type=F32)


def _hilo(x):
    hi = x.astype(BF16)
    return hi, (x - hi.astype(F32)).astype(BF16)


def _dot2(a, b):
    ah = a.astype(BF16)
    bh, bl = _hilo(b)
    return jnp.dot(jnp.concatenate([ah, ah], axis=1), jnp.concatenate([bh, bl], axis=0),
                   preferred_element_type=F32)


def _scan_chunk(r, k, v, kk, lw, icl, ka, states, *, reverse):
    C = RW_CHUNK
    n_pairs = len(states)
    each = range(n_pairs)
    kd = k * (1.0 + (icl - 1.0) * ka)
    a = -kk
    b = kk * icl

    ti = lax.broadcasted_iota(jnp.int32, (C, C), 0)
    si = lax.broadcasted_iota(jnp.int32, (C, C), 1)
    before = (si >= ti) if reverse else (si <= ti)
    lw_hi = lw.astype(BF16)
    lw_rest = lw - lw_hi.astype(F32)
    lw_mid = lw_rest.astype(BF16)
    lw_lo = (lw_rest - lw_mid.astype(F32)).astype(BF16)
    tri = before.astype(BF16)
    cl = jnp.dot(jnp.concatenate([tri, tri, tri], axis=1),
                 jnp.concatenate([lw_hi, lw_mid, lw_lo], axis=0), preferred_element_type=F32)
    tot = cl[0:1, :] if reverse else cl[C - 1:C, :]
    e_neg = jnp.exp(-cl)
    rt = r * jnp.exp(cl)
    at = a * jnp.exp(cl - lw)
    bt = b * e_neg
    kt = kd * e_neg

    lane = lax.broadcasted_iota(jnp.int32, (C, LANES), 1)
    h0 = lane < RW_N
    zero = jnp.zeros((C, LANES), F32)

    def stack(x):
        return jnp.concatenate([jnp.where(h0, x, zero), jnp.where(h0, zero, x)], axis=0)

    def pair(x, p):
        return x[:, p * LANES:(p + 1) * LANES]

    la = [stack(pair(at, p)) for p in each]
    lr = [stack(pair(rt, p)) for p in each]
    vv = [stack(pair(v, p)) for p in each]
    sc = [_dot1(jnp.concatenate([la[p], lr[p]], axis=0),
                jnp.concatenate([pair(bt, p), pair(bt, p), pair(kt, p), pair(kt, p)], axis=0), _NT)
          for p in each]

    row = lax.broadcasted_iota(jnp.int32, (2 * C, 2 * C), 0)
    col = lax.broadcasted_iota(jnp.int32, (2 * C, 2 * C), 1)
    same_head = (row >> _RW_N_BITS) == (col >> _RW_N_BITS)
    strict = same_head & ((col > row) if reverse else (col < row))
    incl = same_head & ((col >= row) if reverse else (col <= row))
    x_ab = [jnp.where(strict, sc[p][0:2 * C, 0:2 * C], 0.0) for p in each]
    x_ak = [jnp.where(strict, sc[p][0:2 * C, 2 * C:4 * C], 0.0) for p in each]
    x_rb = [jnp.where(incl, sc[p][2 * C:4 * C, 0:2 * C], 0.0) for p in each]
    x_rk = [jnp.where(incl, sc[p][2 * C:4 * C, 2 * C:4 * C], 0.0) for p in each]

    def off_mask(bsz):
        bits = (2 * bsz).bit_length() - 1
        blk = (row >> bits) == (col >> bits)
        lo, hi = (row & (2 * bsz - 1)) < bsz, (col & (2 * bsz - 1)) < bsz
        return blk & ((lo & ~hi) if reverse else (~lo & hi))

    eye = (row == col).astype(F32)
    mask1 = off_mask(1)
    tinv = [eye + jnp.where(mask1, x_ab[p], 0.0) for p in each]
    bsz = 2
    while bsz < C:
        mask = off_mask(bsz)
        half = [_dot2(tinv[p], jnp.where(mask, x_ab[p], 0.0)) for p in each]
        tinv = [tinv[p] + _dot2(half[p], tinv[p]) for p in each]
        bsz *= 2

    z = [_dot1(la[p], states[p], _NT) + _dot1(x_ak[p], vv[p]) for p in each]
    u = [_dot2(tinv[p], z[p]) for p in each]
    y = [_dot1(lr[p], states[p], _NT)
         + _dot1(jnp.concatenate([x_rb[p], x_rk[p]], axis=1), jnp.concatenate([u[p], vv[p]], axis=0))
         for p in each]
    upd = [_dot1(jnp.concatenate([u[p][0:C] + u[p][C:2 * C], pair(v, p)], axis=0),
                 jnp.concatenate([pair(bt, p), pair(kt, p)], axis=0), _TN) for p in each]
    decay = jnp.exp(tot)
    new_states = [(states[p] + jnp.where(same_head, upd[p], 0.0)) * pair(decay, p) for p in each]
    ys = [y[p][0:C] + y[p][C:2 * C] for p in each]
    return ys, new_states


def _scan_kernel(r_ref, k_ref, v_ref, kk_ref, lw_ref, icl_ref, ka_ref, s0_ref, y_ref, send_ref,
                 s_scr, *, reverse, n_chunks, pairs):
    c = pl.program_id(1)

    @pl.when(c == 0)
    def _():
        s_scr[...] = s0_ref[...]

    ys, new_states = _scan_chunk(r_ref[...], k_ref[...], v_ref[...], kk_ref[...], lw_ref[...],
                                 icl_ref[...], ka_ref[...], [s_scr[p] for p in range(pairs)],
                                 reverse=reverse)
    for p in range(pairs):
        y_ref[:, p * LANES:(p + 1) * LANES] = ys[p]
        s_scr[p] = new_states[p]

    @pl.when(c == n_chunks - 1)
    def _():
        send_ref[...] = s_scr[...]


def rw_scan(r, k, v, kk, lw, icl, k_a, s0, *, reverse, pairs=8):
    t = r.shape[0]
    n_chunks = t // RW_CHUNK
    n_pairs = RW_W // LANES
    w = pairs * LANES
    if reverse:
        blk = pl.BlockSpec((RW_CHUNK, w), lambda p, c: (n_chunks - 1 - c, p))
    else:
        blk = pl.BlockSpec((RW_CHUNK, w), lambda p, c: (c, p))
    st = pl.BlockSpec((pairs, LANES, LANES), lambda p, c: (p, 0, 0))
    return pl.pallas_call(
        functools.partial(_scan_kernel, reverse=reverse, n_chunks=n_chunks, pairs=pairs),
        grid=(n_pairs // pairs, n_chunks),
        in_specs=[blk] * 6 + [pl.BlockSpec((1, w), lambda p, c: (0, p)), st],
        out_specs=(blk, st),
        out_shape=(jax.ShapeDtypeStruct((t, RW_W), F32),
                   jax.ShapeDtypeStruct((n_pairs, LANES, LANES), F32)),
        scratch_shapes=[pltpu.VMEM((pairs, LANES, LANES), F32)],
        compiler_params=_params("parallel", "arbitrary"),
    )(r, k, v, kk, lw, icl, _row(k_a), s0)


def _rw_post_kernel(yf_ref, yb_ref, r_ref, k_ref, v_ref, iclf_ref, iclb_ref, gate_ref,
                    ka_ref, rk_ref, lnw_ref, lnb_ref, o_ref):
    ones = _head_ones()
    y = yf_ref[...] + yb_ref[...]
    mu = _head_sum(y, ones) * (1.0 / RW_N)
    yc = y - mu
    var = _head_sum(yc * yc, ones) * (1.0 / RW_N)
    yn = yc * lax.rsqrt(var + RW_LN_EPS) * lnw_ref[...] + lnb_ref[...]
    k = k_ref[...]
    ka = ka_ref[...]
    rrk = r_ref[...] * rk_ref[...]
    kd_f = k * (1.0 + (iclf_ref[...] - 1.0) * ka)
    kd_b = k * (1.0 + (iclb_ref[...] - 1.0) * ka)
    bonus = (_head_sum(rrk * kd_f, ones) + _head_sum(rrk * kd_b, ones)) * v_ref[...]
    o_ref[...] = ((yn + bonus) * gate_ref[...]).astype(o_ref.dtype)


def rw_post(yf, yb, r, k, v, icl_f, icl_b, gate, k_a, r_k, ln_w, ln_b, *, tm=256):
    m = yf.shape[0]
    tm = min(tm, m)
    blk = pl.BlockSpec((tm, RW_W), lambda i: (i, 0))
    vec = pl.BlockSpec((1, RW_W), lambda i: (0, 0))
    return pl.pallas_call(
        _rw_post_kernel, grid=(m // tm,),
        in_specs=[blk] * 8 + [vec] * 4, out_specs=blk,
        out_shape=jax.ShapeDtypeStruct((m, RW_W), BF16),
        compiler_params=_params("parallel"),
    )(yf, yb, r, k, v, icl_f, icl_b, gate, _row(k_a), _row(r_k.reshape(-1)), _row(ln_w), _row(ln_b))


def _merge_kernel(ya_ref, yb_ref, yr_ref, yd_ref, zg_ref, bu_ref, gu_ref, gb_ref, o_ref):
    zg = zg_ref[...].astype(BF16)
    acc = None
    for bi, y_ref in enumerate((ya_ref, yb_ref, yr_ref, yd_ref)):
        gate = jax.nn.sigmoid(jnp.dot(zg, gu_ref[bi], preferred_element_type=F32) + gb_ref[bi])
        term = gate * jnp.dot(y_ref[...], bu_ref[bi], preferred_element_type=F32)
        acc = term if acc is None else acc + term
    o_ref[...] = acc.astype(o_ref.dtype)


def merge(ys, zg, branch_up, gate_up, gate_bias, *, tm=1024, tn=512):
    m = zg.shape[0]
    d = branch_up.shape[-1]
    tm, tn = min(tm, m), min(tn, d)
    yblk = pl.BlockSpec((tm, BRANCH_W), lambda i, j: (i, 0))
    return pl.pallas_call(
        _merge_kernel, grid=(m // tm, d // tn),
        in_specs=[yblk] * 4 + [pl.BlockSpec((tm, GATE_R), lambda i, j: (i, 0)),
                               pl.BlockSpec((N_BRANCH, BRANCH_W, tn), lambda i, j: (0, 0, j)),
                               pl.BlockSpec((N_BRANCH, GATE_R, tn), lambda i, j: (0, 0, j)),
                               pl.BlockSpec((N_BRANCH, 1, tn), lambda i, j: (0, 0, j))],
        out_specs=pl.BlockSpec((tm, tn), lambda i, j: (i, j)),
        out_shape=jax.ShapeDtypeStruct((m, d), BF16),
        compiler_params=_params("parallel", "parallel"),
    )(*ys, zg, branch_up, gate_up, gate_bias.reshape(N_BRANCH, 1, d))


def _conv_act_kernel(g_ref, gp_ref, gn_ref, x_ref, xp_ref, xn_ref, wg_ref, wx_ref, o_ref, *, n_blocks):
    def conv(z_ref, zp_ref, zn_ref, w_ref):
        z = z_ref[...]
        prev_row, next_row = _halo_rows(zp_ref, zn_ref, n_blocks)
        zm, zp = _shifted(z, prev_row, next_row)
        return zm * w_ref[0:1, :] + z * w_ref[1:2, :] + zp * w_ref[2:3, :]
    gate = conv(g_ref, gp_ref, gn_ref, wg_ref)
    val = conv(x_ref, xp_ref, xn_ref, wx_ref)
    o_ref[...] = (gate * jax.nn.sigmoid(gate) * val).astype(o_ref.dtype)


def conv_act(hid, w_conv, *, tm=256, tn=1024):
    m, two_f = hid.shape
    f = two_f // 2
    tm, tn = min(tm, m), min(tn, f)
    nb, nj = m // tm, f // tn
    per8 = tm // SUBLANES
    last8 = m // SUBLANES - 1

    def specs(off):
        return [pl.BlockSpec((tm, tn), lambda i, j: (i, j + off)),
                pl.BlockSpec((SUBLANES, tn), lambda i, j: (jnp.maximum(i * per8 - 1, 0), j + off)),
                pl.BlockSpec((SUBLANES, tn), lambda i, j: (jnp.minimum((i + 1) * per8, last8), j + off))]
    wspec = lambda off: pl.BlockSpec((3, tn), lambda i, j: (0, j + off))
    return pl.pallas_call(
        functools.partial(_conv_act_kernel, n_blocks=nb), grid=(nb, nj),
        in_specs=specs(0) + specs(nj) + [wspec(0), wspec(nj)],
        out_specs=pl.BlockSpec((tm, tn), lambda i, j: (i, j)),
        out_shape=jax.ShapeDtypeStruct((m, f), BF16),
        compiler_params=_params("parallel", "parallel"),
    )(hid, hid, hid, hid, hid, hid, w_conv, w_conv)


def diff_mixer(za, zac, lam_vec, subln, lam_init, tabs, need_ctx):
    w = DA_HEADS * HEAD_DIM
    scale = DA_SUB ** -0.5 * LOG2E
    q = prep(za, 0, w, rope=tabs, quarter=DA_SUB // 4, scale=scale)
    k = prep(za, 1, w, rope=tabs, quarter=DA_SUB // 4)
    v = prep(za, 2, w)
    qc = prep(zac, 0, w, scale=scale)
    kc = prep(zac, 1, w)
    vc = prep(zac, 2, w)
    kw = dict(mode="diff", lam=lam_vec, subln=subln, lam_init=lam_init)
    y = flash(q, k, v, kc, vc, tq=512, **kw)
    yc = flash(qc, None, None, kc, vc, tq=256, **kw) if need_ctx else None
    return y, yc


def gqa_mixer(zb, zbc, q_gain, k_gain, tabs, need_ctx):
    wq, wk = GQA_HEADS * HEAD_DIM, GQA_KV * HEAD_DIM
    scale = HEAD_DIM ** -0.5 * LOG2E
    kcol, vcol = wq // wk, wq // wk + 1
    q = prep(zb, 0, wq, gain=q_gain, rope=tabs, quarter=HEAD_DIM // 4, scale=scale)
    k = prep(zb, kcol, wk, gain=k_gain, rope=tabs, quarter=HEAD_DIM // 4)
    v = prep(zb, vcol, wk)
    qc = prep(zbc, 0, wq, gain=q_gain, scale=scale)
    kc = prep(zbc, kcol, wk, gain=k_gain)
    vc = prep(zbc, vcol, wk)
    y = flash(q, k, v, kc, vc, mode="gqa", tq=256)
    yc = flash(qc, None, None, kc, vc, mode="gqa", tq=128) if need_ctx else None
    return y, yc


def swa_mixer(zd, zdc, sink, tabs, need_ctx):
    wq, wk = SWA_HEADS * HEAD_DIM, SWA_KV * HEAD_DIM
    scale = HEAD_DIM ** -0.5 * LOG2E
    kcol, vcol = wq // wk, wq // wk + 1
    q = prep(zd, 0, wq, rope=tabs, quarter=HEAD_DIM // 4, scale=scale)
    k = prep(zd, kcol, wk, rope=tabs, quarter=HEAD_DIM // 4)
    v = prep(zd, vcol, wk)
    qc = prep(zdc, 0, wq, scale=scale)
    kc = prep(zdc, kcol, wk)
    vc = prep(zdc, vcol, wk)
    y = swa(q, k, v, kc, vc, sink)
    yc = flash(qc, None, None, kc, vc, mode="gqa", tq=128, sink=sink) if need_ctx else None
    return y, yc


def rwkv_mixer(zr, zrc, p, need_ctx):
    prep_args = (p["shift"], p["k_k"], p["w0"], p["w_up"], p["a0"], p["a_up"], p["g_up"])
    r, k, v, kk, lwf, lwb, iclf, iclb, gate = rw_prep(zr, *prep_args)
    rc, kc, vc, kkc, lwfc, lwbc, iclfc, iclbc, gatec = rw_prep(zrc, *prep_args)
    s0 = jnp.zeros((RW_W // LANES, LANES, LANES), F32)
    ys, ycs = [], []
    for reverse, lw, icl, lwc, iclc in ((False, lwf, iclf, lwfc, iclfc), (True, lwb, iclb, lwbc, iclbc)):
        yc_d, s_ctx = rw_scan(rc, kc, vc, kkc, lwc, iclc, p["k_a"], s0, reverse=reverse)
        y_d, _ = rw_scan(r, k, v, kk, lw, icl, p["k_a"], s_ctx, reverse=reverse)
        ys.append(y_d)
        ycs.append(yc_d)
    post_args = (p["k_a"], p["r_k"], p["ln_w"], p["ln_b"])
    y = rw_post(ys[0], ys[1], r, k, v, iclf, iclb, gate, *post_args)
    yc = rw_post(ycs[0], ycs[1], rc, kc, vc, iclfc, iclbc, gatec, *post_args) if need_ctx else None
    return y, yc


_SLAB_TILES = ((1024, 512), (1024, 512), (512, 768), (512, 768), (1024, 256))


def kernel(x, c, ctx, c_ctx, mod_down, mod_up, mod_bias, norm_mix_pre, norm_mix_post, norm_ffn_pre, norm_ffn_post, w_in, diff_lambda, diff_subln, gqa_q_norm, gqa_k_norm, rwkv_shift, rwkv_w0, rwkv_w_up, rwkv_a0, rwkv_a_up, rwkv_g_up, rwkv_k_k, rwkv_k_a, rwkv_r_k, rwkv_ln_w, rwkv_ln_b, swa_sink, branch_up, gate_up, gate_bias, w_out, ffn_up, ffn_conv, ffn_down):
    depth = w_in.shape[0]
    s, d = x.shape[1], x.shape[2]
    n_rows = s // GRID_W
    tabs_h = rope_tables(n_rows, HEAD_DIM)
    tabs_s = rope_tables(n_rows, DA_SUB)
    h, hc = x.reshape(s, d), ctx.reshape(ctx.shape[1], d)
    cvec = jnp.zeros((16, d), F32).at[0].set(c[0]).at[1].set(c_ctx)
    mod_rows = []
    for l in range(depth):
        md = matmul(cvec, mod_down[l].astype(BF16), pre_silu=True, tm=16).astype(BF16)
        mods = matmul(md, mod_up[l].astype(BF16), bias=_row(mod_bias[l]), tm=16, tn=2048)
        mod_rows.append((mods[0].reshape(N_MOD, d), mods[1].reshape(N_MOD, d)))
    u = rms_mod(h, norm_mix_pre[0], mod_rows[0][0][0], mod_rows[0][0][1])
    uc = rms_mod(hc, norm_mix_pre[0], mod_rows[0][1][0], mod_rows[0][1][1])
    for l in range(depth):
        need_ctx = l < depth - 1
        lam_init = 0.8 - 0.6 * math.exp(-0.3 * l)
        m, mc = mod_rows[l]
        rw_p = dict(shift=rwkv_shift[l], k_k=rwkv_k_k[l], w0=rwkv_w0[l], w_up=rwkv_w_up[l].astype(BF16),
                    a0=rwkv_a0[l], a_up=rwkv_a_up[l].astype(BF16), g_up=rwkv_g_up[l].astype(BF16),
                    k_a=rwkv_k_a[l], r_k=rwkv_r_k[l], ln_w=rwkv_ln_w[l], ln_b=rwkv_ln_b[l])
        bu, gu = branch_up[l].astype(BF16), gate_up[l].astype(BF16)
        fd = ffn_down[l].astype(BF16)

        def in_proj(a):
            outs, col0 = [], 0
            for width, (tm, tn) in zip(IN_SIZES, _SLAB_TILES):
                outs.append(matmul_w(a, w_in, l, col0, width, tm=tm, tn=tn))
                col0 += width
            return outs

        za, zb, zr, zd, zg = in_proj(u)
        zac, zbc, zrc, zdc, zgc = in_proj(uc)
        ya, yac = diff_mixer(za, zac, diff_lambda[l], diff_subln[l], lam_init, tabs_s, need_ctx)
        yb, ybc = gqa_mixer(zb, zbc, gqa_q_norm[l], gqa_k_norm[l], tabs_h, need_ctx)
        yr, yrc = rwkv_mixer(zr, zrc, rw_p, need_ctx)
        yd, ydc = swa_mixer(zd, zdc, swa_sink[l], tabs_h, need_ctx)

        def sublayers(hh, ys, zgate, mm, mm_next):
            acc = merge(ys, zgate, bu, gu, gate_bias[l])
            mix = matmul_w(acc, w_out, l, 0, d)
            hh, u2 = resid_norm(hh, mix, norm_mix_post[l], mm[2], (norm_ffn_pre[l], mm[3], mm[4]))
            hid = matmul_w(u2, ffn_up, l, 0, ffn_up.shape[2], tm=512, tn=1024)
            act = conv_act(hid, ffn_conv[l])
            f = matmul(act, fd, tm=512)
            if mm_next is None:
                return resid_norm(hh, f, norm_ffn_post[l], mm[5]), None
            return resid_norm(hh, f, norm_ffn_post[l], mm[5],
                              (norm_mix_pre[l + 1], mm_next[0], mm_next[1]))

        h, u = sublayers(h, (ya, yb, yr, yd), zg, m, mod_rows[l + 1][0] if need_ctx else None)
        if need_ctx:
            hc, uc = sublayers(hc, (yac, ybc, yrc, ydc), zgc, mc, mod_rows[l + 1][1])
    return h.reshape(1, s, d)
```

```python
import functools
import math

import jax
import jax.numpy as jnp
from jax import lax
from jax.experimental import pallas as pl
from jax.experimental.pallas import tpu as pltpu

F32 = jnp.float32
BF16 = jnp.bfloat16
HI = lax.Precision.HIGHEST

GRID_W = 64
BLOCK = 128
WINDOW = 128
ROPE_THETA = 10000.0
NORM_EPS = 1e-6
HEAD_DIM = 128
DA_HEADS = 8
DA_SUB = HEAD_DIM // 2
GQA_HEADS = 8
GQA_KV = 2
SWA_HEADS = 8
SWA_KV = 2
RW_HEADS = 16
RW_N = 64
RW_W = RW_HEADS * RW_N
RW_DECAY_R = 128
RW_A_R = 128
RW_GATE_R = 256
RW_LN_EPS = 64e-5
RW_DECAY_SCALE = 0.6065306597126334
N_BRANCH = 4
BRANCH_W = 1024
GATE_R = 256
N_MOD = 6
DA_COLS = 3 * DA_HEADS * HEAD_DIM
GQA_COLS = (GQA_HEADS + 2 * GQA_KV) * HEAD_DIM
RW_COLS = 3 * RW_W + 2 * RW_DECAY_R + 2 * RW_A_R + RW_GATE_R
SWA_COLS = (SWA_HEADS + 2 * SWA_KV) * HEAD_DIM
IN_SIZES = (DA_COLS, GQA_COLS, RW_COLS, SWA_COLS, GATE_R)

LANES = 128
SUBLANES = 8
VMEM_LIMIT = 56 * 1024 * 1024
RW_CHUNK = 64
_RW_N_BITS = RW_N.bit_length() - 1
assert RW_CHUNK == RW_N == 1 << _RW_N_BITS
NEG_BIG = -1e30
LOG2E = math.log2(math.e)

_NT = (((1,), (1,)), ((), ()))
_TN = (((0,), (0,)), ((), ()))


def _params(*sem):
    return pltpu.CompilerParams(dimension_semantics=sem, vmem_limit_bytes=VMEM_LIMIT)


def _row(v):
    return v.reshape(1, -1)


def _mm_kernel(*refs, nk, has_bias, pre_silu):
    a_ref, b_ref = refs[0], refs[1]
    bias_ref = refs[2] if has_bias else None
    o_ref = refs[3] if has_bias else refs[2]
    a = a_ref[...]
    if pre_silu:
        a = (a * jax.nn.sigmoid(a)).astype(BF16)
    prod = jnp.dot(a, b_ref[...], preferred_element_type=F32)

    def finish(acc):
        if has_bias:
            acc = acc + bias_ref[...]
        o_ref[...] = acc.astype(o_ref.dtype)

    if nk == 1:
        finish(prod)
        return
    acc_ref = refs[-1]
    k = pl.program_id(2)

    @pl.when(k == 0)
    def _():
        acc_ref[...] = prod

    @pl.when(k > 0)
    def _():
        acc_ref[...] += prod

    @pl.when(k == nk - 1)
    def _():
        finish(acc_ref[...])


def matmul(a, b, *, out_dtype=F32, tm=1024, tn=512, tk=None, bias=None, pre_silu=False):
    m, kd = a.shape
    n = b.shape[1]
    tm, tn = min(tm, m), min(tn, n)
    tk = kd if tk is None else min(tk, kd)
    assert m % tm == 0 and n % tn == 0 and kd % tk == 0, (a.shape, b.shape, tm, tn, tk)
    nk = kd // tk
    in_specs = [pl.BlockSpec((tm, tk), lambda i, j, k: (i, k)),
                pl.BlockSpec((tk, tn), lambda i, j, k: (k, j))]
    args = [a, b]
    if bias is not None:
        in_specs.append(pl.BlockSpec((1, tn), lambda i, j, k: (0, j)))
        args.append(bias)
    return pl.pallas_call(
        functools.partial(_mm_kernel, nk=nk, has_bias=bias is not None, pre_silu=pre_silu),
        grid=(m // tm, n // tn, nk),
        in_specs=in_specs,
        out_specs=pl.BlockSpec((tm, tn), lambda i, j, k: (i, j)),
        out_shape=jax.ShapeDtypeStruct((m, n), out_dtype),
        scratch_shapes=[pltpu.VMEM((tm, tn), F32)] if nk > 1 else [],
        compiler_params=_params("parallel", "parallel", "arbitrary"),
    )(*args)


def _mmw_kernel(a_ref, w_ref, o_ref, wb_scr):
    @pl.when(pl.program_id(1) == 0)
    def _():
        wb_scr[...] = w_ref[...].astype(BF16)

    o_ref[...] = jnp.dot(a_ref[...], wb_scr[...], preferred_element_type=F32).astype(o_ref.dtype)


def matmul_w(a, w, layer, col0, n, *, out_dtype=F32, tm=1024, tn=512):
    m, kd = a.shape
    tm, tn = min(tm, m), min(tn, n)
    assert m % tm == 0 and n % tn == 0 and col0 % tn == 0 and w.shape[1] == kd
    cb = col0 // tn
    return pl.pallas_call(
        _mmw_kernel, grid=(n // tn, m // tm),
        in_specs=[pl.BlockSpec((tm, kd), lambda j, i: (i, 0)),
                  pl.BlockSpec((None, kd, tn), lambda j, i: (layer, 0, cb + j))],
        out_specs=pl.BlockSpec((tm, tn), lambda j, i: (i, j)),
        out_shape=jax.ShapeDtypeStruct((m, n), out_dtype),
        scratch_shapes=[pltpu.VMEM((kd, tn), BF16)],
        compiler_params=_params("parallel", "arbitrary"),
    )(a, w)


def _rms(x, gain):
    return x * lax.rsqrt(jnp.mean(x * x, axis=-1, keepdims=True) + NORM_EPS) * gain


def _rms_mod_kernel(h_ref, g_ref, sh_ref, sc_ref, o_ref):
    y = _rms(h_ref[...], g_ref[...])
    o_ref[...] = (y * (1.0 + sc_ref[...]) + sh_ref[...]).astype(o_ref.dtype)


def rms_mod(h, gain, shift, scale, *, tm=256):
    m, d = h.shape
    tm = min(tm, m)
    vec = pl.BlockSpec((1, d), lambda i: (0, 0))
    return pl.pallas_call(
        _rms_mod_kernel, grid=(m // tm,),
        in_specs=[pl.BlockSpec((tm, d), lambda i: (i, 0)), vec, vec, vec],
        out_specs=pl.BlockSpec((tm, d), lambda i: (i, 0)),
        out_shape=jax.ShapeDtypeStruct((m, d), BF16),
        compiler_params=_params("parallel"),
    )(h, _row(gain), _row(shift), _row(scale))


def _resid_kernel(*refs, with_next):
    h_ref, y_ref, gp_ref, gate_ref = refs[:4]
    h_new = h_ref[...] + gate_ref[...] * _rms(y_ref[...], gp_ref[...])
    if with_next:
        gn_ref, sh_ref, sc_ref, o_ref, u_ref = refs[4:]
        u = _rms(h_new, gn_ref[...])
        u_ref[...] = (u * (1.0 + sc_ref[...]) + sh_ref[...]).astype(u_ref.dtype)
    else:
        o_ref = refs[4]
    o_ref[...] = h_new


def resid_norm(h, y, gain_post, gate, nxt=None, *, tm=256):
    m, d = h.shape
    tm = min(tm, m)
    vec = pl.BlockSpec((1, d), lambda i: (0, 0))
    blk = pl.BlockSpec((tm, d), lambda i: (i, 0))
    args = [h, y, _row(gain_post), _row(gate)]
    in_specs = [blk, blk, vec, vec]
    out_specs, out_shape = blk, jax.ShapeDtypeStruct((m, d), F32)
    if nxt is not None:
        args += [_row(v) for v in nxt]
        in_specs += [vec, vec, vec]
        out_specs = (blk, blk)
        out_shape = (out_shape, jax.ShapeDtypeStruct((m, d), BF16))
    return pl.pallas_call(
        functools.partial(_resid_kernel, with_next=nxt is not None), grid=(m // tm,),
        in_specs=in_specs, out_specs=out_specs, out_shape=out_shape,
        compiler_params=_params("parallel"),
    )(*args)


def rope_tables(n_rows, dim):
    row = jnp.repeat(jnp.arange(n_rows, dtype=F32), GRID_W)
    col = jnp.tile(jnp.arange(GRID_W, dtype=F32), n_rows)
    quarter = dim // 4
    inv_freq = ROPE_THETA ** (-jnp.arange(quarter, dtype=F32) / quarter)
    ang_r = row[:, None] * inv_freq[None, :]
    ang_c = col[:, None] * inv_freq[None, :]
    ang = jnp.concatenate([ang_r, ang_r, ang_c, ang_c], axis=-1)
    cos, sin = jnp.cos(ang), jnp.sin(ang)
    first = (jnp.arange(dim) % (2 * quarter)) < quarter
    sin_p = jnp.where(first[None, :], -sin, 0.0)
    sin_m = jnp.where(first[None, :], 0.0, sin)
    rep = LANES // dim
    return tuple(jnp.tile(t, (1, rep)) for t in (cos, sin_p, sin_m))


def _prep_qkv_kernel(*refs, widths, has_gain, has_rope, quarter, scale):
    refs = list(refs)
    z_ref = refs.pop(0)
    gains = (refs.pop(0)[...], refs.pop(0)[...]) if has_gain else None
    if has_rope:
        cos = refs.pop(0)[...]
        sin_p = refs.pop(0)[...]
        sin_m = refs.pop(0)[...]
    col = 0
    for which, (o_ref, width) in enumerate(zip(refs, widths)):
        for j in range(width // LANES):
            x = z_ref[:, col + j * LANES:col + (j + 1) * LANES]
            if which < 2:
                if has_gain:
                    x = _rms(x, gains[which])
                if has_rope:
                    x = (x * cos + pltpu.roll(x, LANES - quarter, 1) * sin_p
                         + pltpu.roll(x, quarter, 1) * sin_m)
                if which == 0:
                    x = x * scale
            o_ref[:, j * LANES:(j + 1) * LANES] = x.astype(o_ref.dtype)
        col += width


def prep_qkv(z, widths, *, gains=None, rope=None, quarter=0, scale=1.0, tm=256):
    m, w = z.shape
    assert w == sum(widths)
    tm = min(tm, m)
    args = [z]
    in_specs = [pl.BlockSpec((tm, w), lambda i: (i, 0))]
    if gains is not None:
        args += [_row(g) for g in gains]
        in_specs += [pl.BlockSpec((1, LANES), lambda i: (0, 0))] * 2
    if rope is not None:
        args += list(rope)
        in_specs += [pl.BlockSpec((tm, LANES), lambda i: (i, 0))] * 3
    return pl.pallas_call(
        functools.partial(_prep_qkv_kernel, widths=widths, has_gain=gains is not None,
                          has_rope=rope is not None, quarter=quarter, scale=scale),
        grid=(m // tm,), in_specs=in_specs,
        out_specs=tuple(pl.BlockSpec((tm, wd), lambda i: (i, 0)) for wd in widths),
        out_shape=tuple(jax.ShapeDtypeStruct((m, wd), BF16) for wd in widths),
        compiler_params=_params("parallel"),
    )(*args)


def _flash_kernel(*refs, mode, g, tq, tk, n_chunks, has_sink, lam_init):
    refs = list(refs)
    q_ref, k_ref, v_ref, kc_ref, vc_ref = refs[:5]
    rest = refs[5:]
    if mode == "diff":
        lam_ref, subln_ref, o_ref, m_scr, l_scr, acc_scr, sa_scr, sb_scr = rest
    elif has_sink:
        sink_ref, o_ref, m_scr, l_scr, acc_scr, sa_scr, sb_scr = rest
    else:
        o_ref, m_scr, l_scr, acc_scr, sa_scr, sb_scr = rest

    if mode == "diff":
        q = q_ref[...]
        lane = lax.broadcasted_iota(jnp.int32, q.shape, 1)
        zero = jnp.zeros_like(q)
        qs = jnp.concatenate([jnp.where(lane < DA_SUB, q, zero),
                              jnp.where(lane >= DA_SUB, q, zero)], axis=0)
    else:
        qs = jnp.concatenate([q_ref[:, h * LANES:(h + 1) * LANES] for h in range(g)], axis=0)

    sink_col = None
    if has_sink:
        kv = pl.program_id(0)
        sink_col = jnp.concatenate(
            [jnp.full((tq, 1), sink_ref[kv * g + h] * LOG2E, F32) for h in range(g)], axis=0)

    def with_ones(vals):
        return jnp.concatenate([vals, jnp.ones(vals.shape, BF16)], axis=1)

    s = lax.dot_general(qs, kc_ref[...], _NT, preferred_element_type=F32)
    m = jnp.max(s, axis=-1, keepdims=True)
    if has_sink:
        m = jnp.maximum(m, sink_col)
    pv = jnp.dot(jnp.exp2(s - m).astype(BF16), with_ones(vc_ref[...]), preferred_element_type=F32)
    acc, l_rep = pv[:, :LANES], pv[:, LANES:]

    if n_chunks > 0:
        m_scr[...] = jnp.broadcast_to(m, (g * tq, LANES))
        l_scr[...] = l_rep
        acc_scr[...] = acc

        def scores(c):
            off = pl.multiple_of(c * tk, tk)
            return lax.dot_general(qs, k_ref[pl.ds(off, tk), :], _NT, preferred_element_type=F32)

        def update(s_scr, c):
            off = pl.multiple_of(c * tk, tk)
            vaug = with_ones(v_ref[pl.ds(off, tk), :])
            half = g * tq // 2
            for rows_ in (slice(0, half), slice(half, 2 * half)):
                sb = s_scr[rows_, :]
                m_prev = m_scr[rows_, :]
                m_new = jnp.maximum(m_prev, jnp.max(sb, axis=-1, keepdims=True))
                alpha = jnp.exp2(m_prev - m_new)
                m_rep = jnp.concatenate([m_new] * (tk // LANES), axis=1)
                pv = jnp.dot(jnp.exp2(sb - m_rep).astype(BF16), vaug, preferred_element_type=F32)
                l_scr[rows_, :] = alpha * l_scr[rows_, :] + pv[:, LANES:]
                acc_scr[rows_, :] = alpha * acc_scr[rows_, :] + pv[:, :LANES]
                m_scr[rows_, :] = m_new

        sa_scr[...] = scores(0)

        def body(j, carry):
            c0 = 2 * j
            sb_scr[...] = scores(c0 + 1)
            update(sa_scr, c0)
            sa_scr[...] = scores(jnp.minimum(c0 + 2, n_chunks - 1))
            update(sb_scr, c0 + 1)
            return carry

        lax.fori_loop(0, n_chunks // 2, body, 0)
        m, l_rep, acc = m_scr[:, 0:1], l_scr[...], acc_scr[...]

    l = l_rep[:, 0:1]
    if has_sink:
        l = l + jnp.exp2(sink_col - m)
    out = acc / l
    if mode == "diff":
        lv = lam_ref[...]
        lam = (jnp.exp(jnp.sum(lv[0:1] * lv[1:2], axis=-1, keepdims=True))
               - jnp.exp(jnp.sum(lv[2:3] * lv[3:4], axis=-1, keepdims=True)) + lam_init)
        y = out[:tq] - lam * out[tq:]
        o_ref[...] = (_rms(y, subln_ref[...]) * (1.0 - lam_init)).astype(o_ref.dtype)
    else:
        for h in range(g):
            o_ref[:, h * LANES:(h + 1) * LANES] = out[h * tq:(h + 1) * tq].astype(o_ref.dtype)


def flash(q, k, v, kc, vc, *, mode, tq, tk=1024, sink=None, lam=None, subln=None, lam_init=0.0):
    sq = q.shape[0]
    n_ctx = kc.shape[0]
    if k is None:
        k, v, n_chunks, t_main = kc, vc, 0, n_ctx
    else:
        t_main = k.shape[0]
        tk = min(tk, t_main // 2)
        n_chunks = t_main // tk
        assert n_chunks % 2 == 0 and n_chunks * tk == t_main
    tq = min(tq, sq)
    if mode == "diff":
        n_outer, g, qw = DA_HEADS, 2, LANES
    else:
        n_outer, g = k.shape[1] // LANES, q.shape[1] // k.shape[1]
        qw = g * LANES
    rows = g * tq
    kv_spec = pl.BlockSpec((t_main, LANES), lambda h, i: (0, h))
    ctx_spec = pl.BlockSpec((n_ctx, LANES), lambda h, i: (0, h))
    in_specs = [pl.BlockSpec((tq, qw), lambda h, i: (i, h)), kv_spec, kv_spec, ctx_spec, ctx_spec]
    args = [q, k, v, kc, vc]
    if mode == "diff":
        in_specs += [pl.BlockSpec(lam.shape, lambda h, i: (0, 0)),
                     pl.BlockSpec((1, LANES), lambda h, i: (0, 0))]
        args += [lam, _row(subln)]
    elif sink is not None:
        in_specs.append(pl.BlockSpec(memory_space=pltpu.SMEM))
        args.append(sink)
    return pl.pallas_call(
        functools.partial(_flash_kernel, mode=mode, g=g, tq=tq, tk=tk, n_chunks=n_chunks,
                          has_sink=sink is not None, lam_init=lam_init),
        grid=(n_outer, sq // tq), in_specs=in_specs,
        out_specs=pl.BlockSpec((tq, qw), lambda h, i: (i, h)),
        out_shape=jax.ShapeDtypeStruct((sq, n_outer * qw), BF16),
        scratch_shapes=[pltpu.VMEM((rows, LANES), F32), pltpu.VMEM((rows, LANES), F32),
                        pltpu.VMEM((rows, LANES), F32),
                        pltpu.VMEM((rows, tk), F32), pltpu.VMEM((rows, tk), F32)],
        compiler_params=_params("parallel", "arbitrary"),
    )(*args)


def _swa_kernel(q_ref, kp_ref, kn_ref, kx_ref, vp_ref, vn_ref, vx_ref, kc_ref, vc_ref, sink_ref,
                o_ref, *, g, seq, per_step):
    kv = pl.program_id(0)
    step = pl.program_id(1)
    kwin = jnp.concatenate([kp_ref[...], kn_ref[...], kx_ref[...]], axis=0)
    vwin = jnp.concatenate([vp_ref[...], vn_ref[...], vx_ref[...]], axis=0)
    rows = g * BLOCK
    sink_col = jnp.concatenate(
        [jnp.full((BLOCK, 1), sink_ref[kv * g + h] * LOG2E, F32) for h in range(g)], axis=0)
    row_in_block = lax.broadcasted_iota(jnp.int32, (rows, 3 * BLOCK), 0) & (BLOCK - 1)
    col = lax.broadcasted_iota(jnp.int32, (rows, 3 * BLOCK), 1)
    in_band = jnp.abs(row_in_block + BLOCK - col) <= WINDOW
    blocks = range(per_step)
    qs = [jnp.concatenate([q_ref[b * BLOCK:(b + 1) * BLOCK, h * LANES:(h + 1) * LANES]
                           for h in range(g)], axis=0) for b in blocks]
    s_win = [lax.dot_general(qs[b], kwin[b * BLOCK:(b + 3) * BLOCK], _NT,
                             preferred_element_type=F32) for b in blocks]
    s_ctx = [lax.dot_general(qs[b], kc_ref[...], _NT, preferred_element_type=F32) for b in blocks]
    for b in blocks:
        kpos = (step * per_step + b - 1) * BLOCK + col
        s_win[b] = jnp.where(in_band & (kpos >= 0) & (kpos < seq), s_win[b], NEG_BIG)
    m = [jnp.maximum(jnp.maximum(jnp.max(s_win[b], axis=-1, keepdims=True),
                                 jnp.max(s_ctx[b], axis=-1, keepdims=True)), sink_col)
         for b in blocks]
    p_win = [jnp.exp2(s_win[b] - m[b]) for b in blocks]
    p_ctx = [jnp.exp2(s_ctx[b] - m[b]) for b in blocks]
    l = [jnp.sum(p_win[b], axis=-1, keepdims=True) + jnp.sum(p_ctx[b], axis=-1, keepdims=True)
         + jnp.exp2(sink_col - m[b]) for b in blocks]
    out = [(jnp.dot(p_win[b].astype(BF16), vwin[b * BLOCK:(b + 3) * BLOCK],
                    preferred_element_type=F32)
            + jnp.dot(p_ctx[b].astype(BF16), vc_ref[...], preferred_element_type=F32)) / l[b]
           for b in blocks]
    for b in blocks:
        for h in range(g):
            o_ref[b * BLOCK:(b + 1) * BLOCK, h * LANES:(h + 1) * LANES] = (
                out[b][h * BLOCK:(h + 1) * BLOCK].astype(o_ref.dtype))


def swa(q, k, v, kc, vc, sink, *, per_step=4):
    seq = q.shape[0]
    nb = seq // BLOCK
    per_step = math.gcd(per_step, nb)
    n_kv = k.shape[1] // LANES
    g = q.shape[1] // k.shape[1]
    n_ctx = kc.shape[0]
    prev = pl.BlockSpec((BLOCK, LANES), lambda h, i: (jnp.maximum(i * per_step - 1, 0), h))
    cur = pl.BlockSpec((per_step * BLOCK, LANES), lambda h, i: (i, h))
    nxt = pl.BlockSpec((BLOCK, LANES), lambda h, i: (jnp.minimum((i + 1) * per_step, nb - 1), h))
    ctx_spec = pl.BlockSpec((n_ctx, LANES), lambda h, i: (0, h))
    qspec = pl.BlockSpec((per_step * BLOCK, g * LANES), lambda h, i: (i, h))
    return pl.pallas_call(
        functools.partial(_swa_kernel, g=g, seq=seq, per_step=per_step), grid=(n_kv, nb // per_step),
        in_specs=[qspec, prev, cur, nxt, prev, cur, nxt, ctx_spec, ctx_spec,
                  pl.BlockSpec(memory_space=pltpu.SMEM)],
        out_specs=qspec,
        out_shape=jax.ShapeDtypeStruct(q.shape, BF16),
        compiler_params=_params("parallel", "parallel"),
    )(q, k, k, k, v, v, v, kc, vc, sink)


def _head_ones():
    r = lax.broadcasted_iota(jnp.int32, (LANES, LANES), 0) >> _RW_N_BITS
    c = lax.broadcasted_iota(jnp.int32, (LANES, LANES), 1) >> _RW_N_BITS
    ones = (r == c).astype(BF16)
    return jnp.concatenate([ones, ones, ones], axis=0)


def _head_sum(x, ones3):
    hi = x.astype(BF16)
    rest = x - hi.astype(F32)
    mid = rest.astype(BF16)
    lo = (rest - mid.astype(F32)).astype(BF16)
    parts = []
    for j in range(x.shape[1] // LANES):
        sl = slice(j * LANES, (j + 1) * LANES)
        parts.append(jnp.dot(jnp.concatenate([hi[:, sl], mid[:, sl], lo[:, sl]], axis=1), ones3,
                             preferred_element_type=F32))
    return jnp.concatenate(parts, axis=1)


def _shifted(z, prev_row, next_row):
    tm = z.shape[0]
    rid = lax.broadcasted_iota(jnp.int32, (SUBLANES, z.shape[1]), 0)
    zm, zp = pltpu.roll(z, 1, 0), pltpu.roll(z, tm - 1, 0)
    zm = jnp.concatenate([jnp.where(rid == 0, prev_row, zm[:SUBLANES]), zm[SUBLANES:]], axis=0)
    zp = jnp.concatenate([zp[:tm - SUBLANES],
                          jnp.where(rid == SUBLANES - 1, next_row, zp[tm - SUBLANES:])], axis=0)
    return zm, zp


def _halo_rows(zp_ref, zn_ref, n_blocks):
    i = pl.program_id(0)
    prev_row = jnp.where(i > 0, zp_ref[SUBLANES - 1:SUBLANES, :], 0.0)
    next_row = jnp.where(i < n_blocks - 1, zn_ref[0:1, :], 0.0)
    return prev_row, next_row


def _rw_prep_kernel(z_ref, zp_ref, zn_ref, sw_ref, kk_ref, w0_ref, wup_ref, a0_ref, aup_ref,
                    gup_ref, r_o, k_o, v_o, kk_o, lwf_o, lwb_o, iclf_o, iclb_o, gate_o, *, n_blocks):
    z = z_ref[...]
    prev_row, next_row = _halo_rows(zp_ref, zn_ref, n_blocks)
    zm, zp = _shifted(z, prev_row, next_row)
    t = zm * sw_ref[0:1, :] + z * sw_ref[1:2, :] + zp * sw_ref[2:3, :]
    r_o[...] = t[:, 0:RW_W]
    k = t[:, RW_W:2 * RW_W]
    k_o[...] = k
    v_o[...] = t[:, 2 * RW_W:3 * RW_W]
    kk = k * kk_ref[...]
    ss = _head_sum(kk * kk, _head_ones())
    kk_o[...] = kk * lax.rsqrt(jnp.maximum(ss, 1e-24))
    base = 3 * RW_W
    for d, (lw_o, icl_o) in enumerate(((lwf_o, iclf_o), (lwb_o, iclb_o))):
        xw = t[:, base + d * RW_DECAY_R: base + (d + 1) * RW_DECAY_R]
        pre = w0_ref[d:d + 1, :] + jnp.dot(jnp.tanh(xw).astype(BF16), wup_ref[d],
                                           preferred_element_type=F32)
        lw_o[...] = -RW_DECAY_SCALE * jax.nn.sigmoid(pre)
        a_base = base + 2 * RW_DECAY_R
        xa = t[:, a_base + d * RW_A_R: a_base + (d + 1) * RW_A_R]
        icl_o[...] = jax.nn.sigmoid(a0_ref[d:d + 1, :] + jnp.dot(
            xa.astype(BF16), aup_ref[d], preferred_element_type=F32))
    xg = t[:, base + 2 * RW_DECAY_R + 2 * RW_A_R:]
    gate_o[...] = jnp.dot(jax.nn.sigmoid(xg).astype(BF16), gup_ref[...],
                          preferred_element_type=F32)


def rw_prep(zr, shift_w, k_k, w0, w_up, a0, a_up, g_up, *, tm=128):
    m, w = zr.shape
    tm = min(tm, m)
    nb = m // tm
    per8 = tm // SUBLANES
    last8 = m // SUBLANES - 1
    full = lambda a: pl.BlockSpec(a.shape, lambda i: (0,) * a.ndim)
    out_blk = pl.BlockSpec((tm, RW_W), lambda i: (i, 0))
    args = [zr, zr, zr, shift_w, _row(k_k), w0, w_up, a0, a_up, g_up]
    in_specs = [pl.BlockSpec((tm, w), lambda i: (i, 0)),
                pl.BlockSpec((SUBLANES, w), lambda i: (jnp.maximum(i * per8 - 1, 0), 0)),
                pl.BlockSpec((SUBLANES, w), lambda i: (jnp.minimum((i + 1) * per8, last8), 0)),
                ] + [full(a) for a in args[3:]]
    return pl.pallas_call(
        functools.partial(_rw_prep_kernel, n_blocks=nb), grid=(nb,),
        in_specs=in_specs, out_specs=(out_blk,) * 9,
        out_shape=(jax.ShapeDtypeStruct((m, RW_W), F32),) * 9,
        compiler_params=_params("parallel"),
    )(*args)


_MM = (((1,), (0,)), ((), ()))


def _dot1(a, b, dims=_MM):
    return lax.dot_general(a.astype(BF16), b.astype(BF16), dims, preferred_element_type=F32)


def _hilo(x):
    hi = x.astype(BF16)
    return hi, (x - hi.astype(F32)).astype(BF16)


def _dot2(a, b):
    ah = a.astype(BF16)
    bh, bl = _hilo(b)
    return jnp.dot(jnp.concatenate([ah, ah], axis=1), jnp.concatenate([bh, bl], axis=0),
                   preferred_element_type=F32)


def _scan_chunk(r, k, v, kk, lw, icl, ka, states, *, reverse):
    C = RW_CHUNK
    n_pairs = len(states)
    each = range(n_pairs)
    kd = k * (1.0 + (icl - 1.0) * ka)
    a = -kk
    b = kk * icl

    ti = lax.broadcasted_iota(jnp.int32, (C, C), 0)
    si = lax.broadcasted_iota(jnp.int32, (C, C), 1)
    before = (si >= ti) if reverse else (si <= ti)
    lw_hi = lw.astype(BF16)
    lw_rest = lw - lw_hi.astype(F32)
    lw_mid = lw_rest.astype(BF16)
    lw_lo = (lw_rest - lw_mid.astype(F32)).astype(BF16)
    tri = before.astype(BF16)
    cl = jnp.dot(jnp.concatenate([tri, tri, tri], axis=1),
                 jnp.concatenate([lw_hi, lw_mid, lw_lo], axis=0), preferred_element_type=F32)
    tot = cl[0:1, :] if reverse else cl[C - 1:C, :]
    e_neg = jnp.exp(-cl)
    rt = r * jnp.exp(cl)
    at = a * jnp.exp(cl - lw)
    bt = b * e_neg
    kt = kd * e_neg

    lane = lax.broadcasted_iota(jnp.int32, (C, LANES), 1)
    h0 = lane < RW_N
    zero = jnp.zeros((C, LANES), F32)

    def stack(x):
        return jnp.concatenate([jnp.where(h0, x, zero), jnp.where(h0, zero, x)], axis=0)

    def pair(x, p):
        return x[:, p * LANES:(p + 1) * LANES]

    la = [stack(pair(at, p)) for p in each]
    lr = [stack(pair(rt, p)) for p in each]
    vv = [stack(pair(v, p)) for p in each]
    sc = [_dot1(jnp.concatenate([la[p], lr[p]], axis=0),
                jnp.concatenate([pair(bt, p), pair(bt, p), pair(kt, p), pair(kt, p)], axis=0), _NT)
          for p in each]

    row = lax.broadcasted_iota(jnp.int32, (2 * C, 2 * C), 0)
    col = lax.broadcasted_iota(jnp.int32, (2 * C, 2 * C), 1)
    same_head = (row >> _RW_N_BITS) == (col >> _RW_N_BITS)
    strict = same_head & ((col > row) if reverse else (col < row))
    incl = same_head & ((col >= row) if reverse else (col <= row))
    x_ab = [jnp.where(strict, sc[p][0:2 * C, 0:2 * C], 0.0) for p in each]
    x_ak = [jnp.where(strict, sc[p][0:2 * C, 2 * C:4 * C], 0.0) for p in each]
    x_rb = [jnp.where(incl, sc[p][2 * C:4 * C, 0:2 * C], 0.0) for p in each]
    x_rk = [jnp.where(incl, sc[p][2 * C:4 * C, 2 * C:4 * C], 0.0) for p in each]

    def off_mask(bsz):
        bits = (2 * bsz).bit_length() - 1
        blk = (row >> bits) == (col >> bits)
        lo, hi = (row & (2 * bsz - 1)) < bsz, (col & (2 * bsz - 1)) < bsz
        return blk & ((lo & ~hi) if reverse else (~lo & hi))

    eye = (row == col).astype(F32)
    mask1 = off_mask(1)
    tinv = [eye + jnp.where(mask1, x_ab[p], 0.0) for p in each]
    bsz = 2
    while bsz < C:
        mask = off_mask(bsz)
        half = [_dot2(tinv[p], jnp.where(mask, x_ab[p], 0.0)) for p in each]
        tinv = [tinv[p] + _dot2(half[p], tinv[p]) for p in each]
        bsz *= 2

    z = [_dot1(la[p], states[p], _NT) + _dot1(x_ak[p], vv[p]) for p in each]
    u = [_dot2(tinv[p], z[p]) for p in each]
    y = [_dot1(lr[p], states[p], _NT)
         + _dot1(jnp.concatenate([x_rb[p], x_rk[p]], axis=1), jnp.concatenate([u[p], vv[p]], axis=0))
         for p in each]
    upd = [_dot1(jnp.concatenate([u[p][0:C] + u[p][C:2 * C], pair(v, p)], axis=0),
                 jnp.concatenate([pair(bt, p), pair(kt, p)], axis=0), _TN) for p in each]
    decay = jnp.exp(tot)
    new_states = [(states[p] + jnp.where(same_head, upd[p], 0.0)) * pair(decay, p) for p in each]
    ys = [y[p][0:C] + y[p][C:2 * C] for p in each]
    return ys, new_states


def _scan_kernel(r_ref, k_ref, v_ref, kk_ref, lw_ref, icl_ref, ka_ref, s0_ref, y_ref, send_ref,
                 s_scr, *, reverse, n_chunks, pairs):
    c = pl.program_id(1)

    @pl.when(c == 0)
    def _():
        s_scr[...] = s0_ref[...]

    ys, new_states = _scan_chunk(r_ref[...], k_ref[...], v_ref[...], kk_ref[...], lw_ref[...],
                                 icl_ref[...], ka_ref[...], [s_scr[p] for p in range(pairs)],
                                 reverse=reverse)
    for p in range(pairs):
        y_ref[:, p * LANES:(p + 1) * LANES] = ys[p]
        s_scr[p] = new_states[p]

    @pl.when(c == n_chunks - 1)
    def _():
        send_ref[...] = s_scr[...]


def rw_scan(r, k, v, kk, lw, icl, k_a, s0, *, reverse, pairs=8):
    t = r.shape[0]
    n_chunks = t // RW_CHUNK
    n_pairs = RW_W // LANES
    w = pairs * LANES
    if reverse:
        blk = pl.BlockSpec((RW_CHUNK, w), lambda p, c: (n_chunks - 1 - c, p))
    else:
        blk = pl.BlockSpec((RW_CHUNK, w), lambda p, c: (c, p))
    st = pl.BlockSpec((pairs, LANES, LANES), lambda p, c: (p, 0, 0))
    return pl.pallas_call(
        functools.partial(_scan_kernel, reverse=reverse, n_chunks=n_chunks, pairs=pairs),
        grid=(n_pairs // pairs, n_chunks),
        in_specs=[blk] * 6 + [pl.BlockSpec((1, w), lambda p, c: (0, p)), st],
        out_specs=(blk, st),
        out_shape=(jax.ShapeDtypeStruct((t, RW_W), F32),
                   jax.ShapeDtypeStruct((n_pairs, LANES, LANES), F32)),
        scratch_shapes=[pltpu.VMEM((pairs, LANES, LANES), F32)],
        compiler_params=_params("parallel", "arbitrary"),
    )(r, k, v, kk, lw, icl, _row(k_a), s0)


def _rw_post_kernel(yf_ref, yb_ref, r_ref, k_ref, v_ref, iclf_ref, iclb_ref, gate_ref,
                    ka_ref, rk_ref, lnw_ref, lnb_ref, o_ref):
    ones = _head_ones()
    y = yf_ref[...] + yb_ref[...]
    mu = _head_sum(y, ones) * (1.0 / RW_N)
    yc = y - mu
    var = _head_sum(yc * yc, ones) * (1.0 / RW_N)
    yn = yc * lax.rsqrt(var + RW_LN_EPS) * lnw_ref[...] + lnb_ref[...]
    k = k_ref[...]
    ka = ka_ref[...]
    rrk = r_ref[...] * rk_ref[...]
    kd_f = k * (1.0 + (iclf_ref[...] - 1.0) * ka)
    kd_b = k * (1.0 + (iclb_ref[...] - 1.0) * ka)
    bonus = (_head_sum(rrk * kd_f, ones) + _head_sum(rrk * kd_b, ones)) * v_ref[...]
    o_ref[...] = ((yn + bonus) * gate_ref[...]).astype(o_ref.dtype)


def rw_post(yf, yb, r, k, v, icl_f, icl_b, gate, k_a, r_k, ln_w, ln_b, *, tm=256):
    m = yf.shape[0]
    tm = min(tm, m)
    blk = pl.BlockSpec((tm, RW_W), lambda i: (i, 0))
    vec = pl.BlockSpec((1, RW_W), lambda i: (0, 0))
    return pl.pallas_call(
        _rw_post_kernel, grid=(m // tm,),
        in_specs=[blk] * 8 + [vec] * 4, out_specs=blk,
        out_shape=jax.ShapeDtypeStruct((m, RW_W), BF16),
        compiler_params=_params("parallel"),
    )(yf, yb, r, k, v, icl_f, icl_b, gate, _row(k_a), _row(r_k.reshape(-1)), _row(ln_w), _row(ln_b))


def _merge_kernel(ya_ref, yb_ref, yr_ref, yd_ref, zg_ref, bu_ref, gu_ref, gb_ref, o_ref):
    zg = zg_ref[...].astype(BF16)
    acc = None
    for bi, y_ref in enumerate((ya_ref, yb_ref, yr_ref, yd_ref)):
        gate = jax.nn.sigmoid(jnp.dot(zg, gu_ref[bi], preferred_element_type=F32) + gb_ref[bi])
        term = gate * jnp.dot(y_ref[...], bu_ref[bi], preferred_element_type=F32)
        acc = term if acc is None else acc + term
    o_ref[...] = acc.astype(o_ref.dtype)


def merge(ys, zg, branch_up, gate_up, gate_bias, *, tm=1024, tn=512):
    m = zg.shape[0]
    d = branch_up.shape[-1]
    tm, tn = min(tm, m), min(tn, d)
    yblk = pl.BlockSpec((tm, BRANCH_W), lambda i, j: (i, 0))
    return pl.pallas_call(
        _merge_kernel, grid=(m // tm, d // tn),
        in_specs=[yblk] * 4 + [pl.BlockSpec((tm, GATE_R), lambda i, j: (i, 0)),
                               pl.BlockSpec((N_BRANCH, BRANCH_W, tn), lambda i, j: (0, 0, j)),
                               pl.BlockSpec((N_BRANCH, GATE_R, tn), lambda i, j: (0, 0, j)),
                               pl.BlockSpec((N_BRANCH, 1, tn), lambda i, j: (0, 0, j))],
        out_specs=pl.BlockSpec((tm, tn), lambda i, j: (i, j)),
        out_shape=jax.ShapeDtypeStruct((m, d), BF16),
        compiler_params=_params("parallel", "parallel"),
    )(*ys, zg, branch_up, gate_up, gate_bias.reshape(N_BRANCH, 1, d))


def _conv_act_kernel(g_ref, gp_ref, gn_ref, x_ref, xp_ref, xn_ref, wg_ref, wx_ref, o_ref, *, n_blocks):
    def conv(z_ref, zp_ref, zn_ref, w_ref):
        z = z_ref[...]
        prev_row, next_row = _halo_rows(zp_ref, zn_ref, n_blocks)
        zm, zp = _shifted(z, prev_row, next_row)
        return zm * w_ref[0:1, :] + z * w_ref[1:2, :] + zp * w_ref[2:3, :]
    gate = conv(g_ref, gp_ref, gn_ref, wg_ref)
    val = conv(x_ref, xp_ref, xn_ref, wx_ref)
    o_ref[...] = (gate * jax.nn.sigmoid(gate) * val).astype(o_ref.dtype)


def conv_act(hid, w_conv, *, tm=256, tn=1024):
    m, two_f = hid.shape
    f = two_f // 2
    tm, tn = min(tm, m), min(tn, f)
    nb, nj = m // tm, f // tn
    per8 = tm // SUBLANES
    last8 = m // SUBLANES - 1

    def specs(off):
        return [pl.BlockSpec((tm, tn), lambda i, j: (i, j + off)),
                pl.BlockSpec((SUBLANES, tn), lambda i, j: (jnp.maximum(i * per8 - 1, 0), j + off)),
                pl.BlockSpec((SUBLANES, tn), lambda i, j: (jnp.minimum((i + 1) * per8, last8), j + off))]
    wspec = lambda off: pl.BlockSpec((3, tn), lambda i, j: (0, j + off))
    return pl.pallas_call(
        functools.partial(_conv_act_kernel, n_blocks=nb), grid=(nb, nj),
        in_specs=specs(0) + specs(nj) + [wspec(0), wspec(nj)],
        out_specs=pl.BlockSpec((tm, tn), lambda i, j: (i, j)),
        out_shape=jax.ShapeDtypeStruct((m, f), BF16),
        compiler_params=_params("parallel", "parallel"),
    )(hid, hid, hid, hid, hid, hid, w_conv, w_conv)


def diff_mixer(za, zac, lam_vec, subln, lam_init, tabs, need_ctx):
    w = DA_HEADS * HEAD_DIM
    scale = DA_SUB ** -0.5 * LOG2E
    q, k, v = prep_qkv(za, (w, w, w), rope=tabs, quarter=DA_SUB // 4, scale=scale)
    qc, kc, vc = prep_qkv(zac, (w, w, w), scale=scale)
    kw = dict(mode="diff", lam=lam_vec, subln=subln, lam_init=lam_init)
    y = flash(q, k, v, kc, vc, tq=1024, **kw)
    yc = flash(qc, None, None, kc, vc, tq=256, **kw) if need_ctx else None
    return y, yc


def gqa_mixer(zb, zbc, q_gain, k_gain, tabs, need_ctx):
    wq, wk = GQA_HEADS * HEAD_DIM, GQA_KV * HEAD_DIM
    scale = HEAD_DIM ** -0.5 * LOG2E
    gains = (q_gain, k_gain)
    q, k, v = prep_qkv(zb, (wq, wk, wk), gains=gains, rope=tabs, quarter=HEAD_DIM // 4, scale=scale)
    qc, kc, vc = prep_qkv(zbc, (wq, wk, wk), gains=gains, scale=scale)
    y = flash(q, k, v, kc, vc, mode="gqa", tq=512)
    yc = flash(qc, None, None, kc, vc, mode="gqa", tq=128) if need_ctx else None
    return y, yc


def swa_mixer(zd, zdc, sink, tabs, need_ctx):
    wq, wk = SWA_HEADS * HEAD_DIM, SWA_KV * HEAD_DIM
    scale = HEAD_DIM ** -0.5 * LOG2E
    q, k, v = prep_qkv(zd, (wq, wk, wk), rope=tabs, quarter=HEAD_DIM // 4, scale=scale)
    qc, kc, vc = prep_qkv(zdc, (wq, wk, wk), scale=scale)
    y = swa(q, k, v, kc, vc, sink)
    yc = flash(qc, None, None, kc, vc, mode="gqa", tq=128, sink=sink) if need_ctx else None
    return y, yc


def rwkv_mixer(zr, zrc, p, need_ctx):
    prep_args = (p["shift"], p["k_k"], p["w0"], p["w_up"], p["a0"], p["a_up"], p["g_up"])
    r, k, v, kk, lwf, lwb, iclf, iclb, gate = rw_prep(zr, *prep_args)
    rc, kc, vc, kkc, lwfc, lwbc, iclfc, iclbc, gatec = rw_prep(zrc, *prep_args)
    s0 = jnp.zeros((RW_W // LANES, LANES, LANES), F32)
    ys, ycs = [], []
    for reverse, lw, icl, lwc, iclc in ((False, lwf, iclf, lwfc, iclfc), (True, lwb, iclb, lwbc, iclbc)):
        yc_d, s_ctx = rw_scan(rc, kc, vc, kkc, lwc, iclc, p["k_a"], s0, reverse=reverse)
        y_d, _ = rw_scan(r, k, v, kk, lw, icl, p["k_a"], s_ctx, reverse=reverse)
        ys.append(y_d)
        ycs.append(yc_d)
    post_args = (p["k_a"], p["r_k"], p["ln_w"], p["ln_b"])
    y = rw_post(ys[0], ys[1], r, k, v, iclf, iclb, gate, *post_args)
    yc = rw_post(ycs[0], ycs[1], rc, kc, vc, iclfc, iclbc, gatec, *post_args) if need_ctx else None
    return y, yc


_SLAB_TILES = ((1024, 512), (1024, 512), (512, 768), (512, 768), (1024, 256))


def kernel(x, c, ctx, c_ctx, mod_down, mod_up, mod_bias, norm_mix_pre, norm_mix_post, norm_ffn_pre, norm_ffn_post, w_in, diff_lambda, diff_subln, gqa_q_norm, gqa_k_norm, rwkv_shift, rwkv_w0, rwkv_w_up, rwkv_a0, rwkv_a_up, rwkv_g_up, rwkv_k_k, rwkv_k_a, rwkv_r_k, rwkv_ln_w, rwkv_ln_b, swa_sink, branch_up, gate_up, gate_bias, w_out, ffn_up, ffn_conv, ffn_down):
    depth = w_in.shape[0]
    s, d = x.shape[1], x.shape[2]
    n_rows = s // GRID_W
    tabs_h = rope_tables(n_rows, HEAD_DIM)
    tabs_s = rope_tables(n_rows, DA_SUB)
    h, hc = x.reshape(s, d), ctx.reshape(ctx.shape[1], d)
    cvec = jnp.zeros((16, d), F32).at[0].set(c[0]).at[1].set(c_ctx)
    mod_rows = []
    for l in range(depth):
        md = matmul(cvec, mod_down[l].astype(BF16), pre_silu=True, tm=16).astype(BF16)
        mods = matmul(md, mod_up[l].astype(BF16), bias=_row(mod_bias[l]), tm=16, tn=2048)
        mod_rows.append((mods[0].reshape(N_MOD, d), mods[1].reshape(N_MOD, d)))
    u = rms_mod(h, norm_mix_pre[0], mod_rows[0][0][0], mod_rows[0][0][1])
    uc = rms_mod(hc, norm_mix_pre[0], mod_rows[0][1][0], mod_rows[0][1][1])
    for l in range(depth):
        need_ctx = l < depth - 1
        lam_init = 0.8 - 0.6 * math.exp(-0.3 * l)
        m, mc = mod_rows[l]
        rw_p = dict(shift=rwkv_shift[l], k_k=rwkv_k_k[l], w0=rwkv_w0[l], w_up=rwkv_w_up[l].astype(BF16),
                    a0=rwkv_a0[l], a_up=rwkv_a_up[l].astype(BF16), g_up=rwkv_g_up[l].astype(BF16),
                    k_a=rwkv_k_a[l], r_k=rwkv_r_k[l], ln_w=rwkv_ln_w[l], ln_b=rwkv_ln_b[l])
        bu, gu = branch_up[l].astype(BF16), gate_up[l].astype(BF16)
        fd = ffn_down[l].astype(BF16)

        def in_proj(a):
            outs, col0 = [], 0
            for width, (tm, tn) in zip(IN_SIZES, _SLAB_TILES):
                outs.append(matmul_w(a, w_in, l, col0, width, tm=tm, tn=tn))
                col0 += width
            return outs

        za, zb, zr, zd, zg = in_proj(u)
        zac, zbc, zrc, zdc, zgc = in_proj(uc)
        ya, yac = diff_mixer(za, zac, diff_lambda[l], diff_subln[l], lam_init, tabs_s, need_ctx)
        yb, ybc = gqa_mixer(zb, zbc, gqa_q_norm[l], gqa_k_norm[l], tabs_h, need_ctx)
        yr, yrc = rwkv_mixer(zr, zrc, rw_p, need_ctx)
        yd, ydc = swa_mixer(zd, zdc, swa_sink[l], tabs_h, need_ctx)

        def sublayers(hh, ys, zgate, mm, mm_next):
            acc = merge(ys, zgate, bu, gu, gate_bias[l])
            mix = matmul_w(acc, w_out, l, 0, d)
            hh, u2 = resid_norm(hh, mix, norm_mix_post[l], mm[2], (norm_ffn_pre[l], mm[3], mm[4]))
            hid = matmul_w(u2, ffn_up, l, 0, ffn_up.shape[2], tm=512, tn=1024)
            act = conv_act(hid, ffn_conv[l])
            f = matmul(act, fd, tm=512)
            if mm_next is None:
                return resid_norm(hh, f, norm_ffn_post[l], mm[5]), None
            return resid_norm(hh, f, norm_ffn_post[l], mm[5],
                              (norm_mix_pre[l + 1], mm_next[0], mm_next[1]))

        h, u = sublayers(h, (ya, yb, yr, yd), zg, m, mod_rows[l + 1][0] if need_ctx else None)
        if need_ctx:
            hc, uc = sublayers(hc, (yac, ybc, yrc, ydc), zgc, mc, mod_rows[l + 1][1])
    return h.reshape(1, s, d)
```

```python
import functools
import math

import jax
import jax.numpy as jnp
from jax import lax
from jax.experimental import pallas as pl
from jax.experimental.pallas import tpu as pltpu

F32 = jnp.float32
BF16 = jnp.bfloat16

GRID_W = 64
BLOCK = 128
WINDOW = 128
ROPE_THETA = 10000.0
NORM_EPS = 1e-6
HEAD_DIM = 128
DA_HEADS = 8
DA_SUB = HEAD_DIM // 2
GQA_HEADS = 8
GQA_KV = 2
SWA_HEADS = 8
SWA_KV = 2
RW_HEADS = 16
RW_N = 64
RW_W = RW_HEADS * RW_N
RW_DECAY_R = 128
RW_A_R = 128
RW_GATE_R = 256
RW_LN_EPS = 64e-5
RW_DECAY_SCALE = 0.6065306597126334
N_BRANCH = 4
BRANCH_W = 1024
GATE_R = 256
N_MOD = 6
DA_COLS = 3 * DA_HEADS * HEAD_DIM
GQA_COLS = (GQA_HEADS + 2 * GQA_KV) * HEAD_DIM
RW_COLS = 3 * RW_W + 2 * RW_DECAY_R + 2 * RW_A_R + RW_GATE_R
SWA_COLS = (SWA_HEADS + 2 * SWA_KV) * HEAD_DIM
IN_SIZES = (DA_COLS, GQA_COLS, RW_COLS, SWA_COLS, GATE_R)

LANES = 128
SUBLANES = 8
VMEM_LIMIT = 56 * 1024 * 1024
RW_CHUNK = 64
_RW_N_BITS = RW_N.bit_length() - 1
assert RW_CHUNK == RW_N == 1 << _RW_N_BITS
NEG_BIG = -1e30
LOG2E = math.log2(math.e)

_NT = (((1,), (1,)), ((), ()))
_TN = (((0,), (0,)), ((), ()))


def _params(*sem):
    return pltpu.CompilerParams(dimension_semantics=sem, vmem_limit_bytes=VMEM_LIMIT)


def _row(v):
    return v.reshape(1, -1)


def _mm_kernel(*refs, nk, has_bias, pre_silu):
    a_ref, b_ref = refs[0], refs[1]
    bias_ref = refs[2] if has_bias else None
    o_ref = refs[3] if has_bias else refs[2]
    a = a_ref[...]
    if pre_silu:
        a = (a * jax.nn.sigmoid(a)).astype(BF16)
    prod = jnp.dot(a, b_ref[...], preferred_element_type=F32)

    def finish(acc):
        if has_bias:
            acc = acc + bias_ref[...]
        o_ref[...] = acc.astype(o_ref.dtype)

    if nk == 1:
        finish(prod)
        return
    acc_ref = refs[-1]
    k = pl.program_id(2)

    @pl.when(k == 0)
    def _():
        acc_ref[...] = prod

    @pl.when(k > 0)
    def _():
        acc_ref[...] += prod

    @pl.when(k == nk - 1)
    def _():
        finish(acc_ref[...])


def matmul(a, b, *, out_dtype=F32, tm=1024, tn=512, tk=None, bias=None, pre_silu=False):
    m, kd = a.shape
    n = b.shape[1]
    tm, tn = min(tm, m), min(tn, n)
    tk = kd if tk is None else min(tk, kd)
    assert m % tm == 0 and n % tn == 0 and kd % tk == 0, (a.shape, b.shape, tm, tn, tk)
    nk = kd // tk
    in_specs = [pl.BlockSpec((tm, tk), lambda i, j, k: (i, k)),
                pl.BlockSpec((tk, tn), lambda i, j, k: (k, j))]
    args = [a, b]
    if bias is not None:
        in_specs.append(pl.BlockSpec((1, tn), lambda i, j, k: (0, j)))
        args.append(bias)
    return pl.pallas_call(
        functools.partial(_mm_kernel, nk=nk, has_bias=bias is not None, pre_silu=pre_silu),
        grid=(m // tm, n // tn, nk),
        in_specs=in_specs,
        out_specs=pl.BlockSpec((tm, tn), lambda i, j, k: (i, j)),
        out_shape=jax.ShapeDtypeStruct((m, n), out_dtype),
        scratch_shapes=[pltpu.VMEM((tm, tn), F32)] if nk > 1 else [],
        compiler_params=_params("parallel", "parallel", "arbitrary"),
    )(*args)


def _mmw_kernel(a_ref, w_ref, o_ref, wb_scr):
    @pl.when(pl.program_id(1) == 0)
    def _():
        wb_scr[...] = w_ref[...].astype(BF16)

    o_ref[...] = jnp.dot(a_ref[...], wb_scr[...], preferred_element_type=F32).astype(o_ref.dtype)


def matmul_w(a, w, layer, col0, n, *, out_dtype=F32, tm=1024, tn=512):
    m, kd = a.shape
    tm, tn = min(tm, m), min(tn, n)
    assert m % tm == 0 and n % tn == 0 and col0 % tn == 0 and w.shape[1] == kd
    cb = col0 // tn
    return pl.pallas_call(
        _mmw_kernel, grid=(n // tn, m // tm),
        in_specs=[pl.BlockSpec((tm, kd), lambda j, i: (i, 0)),
                  pl.BlockSpec((None, kd, tn), lambda j, i: (layer, 0, cb + j))],
        out_specs=pl.BlockSpec((tm, tn), lambda j, i: (i, j)),
        out_shape=jax.ShapeDtypeStruct((m, n), out_dtype),
        scratch_shapes=[pltpu.VMEM((kd, tn), BF16)],
        compiler_params=_params("parallel", "arbitrary"),
    )(a, w)


def _rms(x, gain):
    return x * lax.rsqrt(jnp.mean(x * x, axis=-1, keepdims=True) + NORM_EPS) * gain


def _rms_mod_kernel(h_ref, g_ref, sh_ref, sc_ref, o_ref):
    y = _rms(h_ref[...], g_ref[...])
    o_ref[...] = (y * (1.0 + sc_ref[...]) + sh_ref[...]).astype(o_ref.dtype)


def rms_mod(h, gain, shift, scale, *, tm=256):
    m, d = h.shape
    tm = min(tm, m)
    vec = pl.BlockSpec((1, d), lambda i: (0, 0))
    return pl.pallas_call(
        _rms_mod_kernel, grid=(m // tm,),
        in_specs=[pl.BlockSpec((tm, d), lambda i: (i, 0)), vec, vec, vec],
        out_specs=pl.BlockSpec((tm, d), lambda i: (i, 0)),
        out_shape=jax.ShapeDtypeStruct((m, d), BF16),
        compiler_params=_params("parallel"),
    )(h, _row(gain), _row(shift), _row(scale))


def _resid_kernel(*refs, with_next):
    h_ref, y_ref, gp_ref, gate_ref = refs[:4]
    h_new = h_ref[...] + gate_ref[...] * _rms(y_ref[...], gp_ref[...])
    if with_next:
        gn_ref, sh_ref, sc_ref, o_ref, u_ref = refs[4:]
        u = _rms(h_new, gn_ref[...])
        u_ref[...] = (u * (1.0 + sc_ref[...]) + sh_ref[...]).astype(u_ref.dtype)
    else:
        o_ref = refs[4]
    o_ref[...] = h_new


def resid_norm(h, y, gain_post, gate, nxt=None, *, tm=256):
    m, d = h.shape
    tm = min(tm, m)
    vec = pl.BlockSpec((1, d), lambda i: (0, 0))
    blk = pl.BlockSpec((tm, d), lambda i: (i, 0))
    args = [h, y, _row(gain_post), _row(gate)]
    in_specs = [blk, blk, vec, vec]
    out_specs, out_shape = blk, jax.ShapeDtypeStruct((m, d), F32)
    if nxt is not None:
        args += [_row(v) for v in nxt]
        in_specs += [vec, vec, vec]
        out_specs = (blk, blk)
        out_shape = (out_shape, jax.ShapeDtypeStruct((m, d), BF16))
    return pl.pallas_call(
        functools.partial(_resid_kernel, with_next=nxt is not None), grid=(m // tm,),
        in_specs=in_specs, out_specs=out_specs, out_shape=out_shape,
        compiler_params=_params("parallel"),
    )(*args)


def rope_tables(n_rows, dim):
    row = jnp.repeat(jnp.arange(n_rows, dtype=F32), GRID_W)
    col = jnp.tile(jnp.arange(GRID_W, dtype=F32), n_rows)
    quarter = dim // 4
    inv_freq = ROPE_THETA ** (-jnp.arange(quarter, dtype=F32) / quarter)
    ang_r = row[:, None] * inv_freq[None, :]
    ang_c = col[:, None] * inv_freq[None, :]
    ang = jnp.concatenate([ang_r, ang_r, ang_c, ang_c], axis=-1)
    cos, sin = jnp.cos(ang), jnp.sin(ang)
    first = (jnp.arange(dim) % (2 * quarter)) < quarter
    sin_p = jnp.where(first[None, :], -sin, 0.0)
    sin_m = jnp.where(first[None, :], 0.0, sin)
    rep = LANES // dim
    return tuple(jnp.tile(t, (1, rep)) for t in (cos, sin_p, sin_m))


def _prep_qkv_kernel(*refs, widths, has_gain, has_rope, quarter, scale):
    refs = list(refs)
    z_ref = refs.pop(0)
    gains = (refs.pop(0)[...], refs.pop(0)[...]) if has_gain else None
    if has_rope:
        cos = refs.pop(0)[...]
        sin_p = refs.pop(0)[...]
        sin_m = refs.pop(0)[...]
    col = 0
    for which, (o_ref, width) in enumerate(zip(refs, widths)):
        for j in range(width // LANES):
            x = z_ref[:, col + j * LANES:col + (j + 1) * LANES]
            if which < 2:
                if has_gain:
                    x = _rms(x, gains[which])
                if has_rope:
                    x = (x * cos + pltpu.roll(x, LANES - quarter, 1) * sin_p
                         + pltpu.roll(x, quarter, 1) * sin_m)
                if which == 0:
                    x = x * scale
            o_ref[:, j * LANES:(j + 1) * LANES] = x.astype(o_ref.dtype)
        col += width


def prep_qkv(z, widths, *, gains=None, rope=None, quarter=0, scale=1.0, tm=256):
    m, w = z.shape
    assert w == sum(widths)
    tm = min(tm, m)
    args = [z]
    in_specs = [pl.BlockSpec((tm, w), lambda i: (i, 0))]
    if gains is not None:
        args += [_row(g) for g in gains]
        in_specs += [pl.BlockSpec((1, LANES), lambda i: (0, 0))] * 2
    if rope is not None:
        args += list(rope)
        in_specs += [pl.BlockSpec((tm, LANES), lambda i: (i, 0))] * 3
    return pl.pallas_call(
        functools.partial(_prep_qkv_kernel, widths=widths, has_gain=gains is not None,
                          has_rope=rope is not None, quarter=quarter, scale=scale),
        grid=(m // tm,), in_specs=in_specs,
        out_specs=tuple(pl.BlockSpec((tm, wd), lambda i: (i, 0)) for wd in widths),
        out_shape=tuple(jax.ShapeDtypeStruct((m, wd), BF16) for wd in widths),
        compiler_params=_params("parallel"),
    )(*args)


def _flash_kernel(*refs, mode, g, tq, tk, n_chunks, has_sink, lam_init):
    refs = list(refs)
    q_ref, k_ref, v_ref, kc_ref, vc_ref = refs[:5]
    rest = refs[5:]
    if mode == "diff":
        lam_ref, subln_ref, o_ref, m_scr, l_scr, acc_scr, sa_scr, sb_scr = rest
    elif has_sink:
        sink_ref, o_ref, m_scr, l_scr, acc_scr, sa_scr, sb_scr = rest
    else:
        o_ref, m_scr, l_scr, acc_scr, sa_scr, sb_scr = rest

    if mode == "diff":
        q = q_ref[...]
        lane = lax.broadcasted_iota(jnp.int32, q.shape, 1)
        zero = jnp.zeros_like(q)
        qs = jnp.concatenate([jnp.where(lane < DA_SUB, q, zero),
                              jnp.where(lane >= DA_SUB, q, zero)], axis=0)
    else:
        qs = jnp.concatenate([q_ref[:, h * LANES:(h + 1) * LANES] for h in range(g)], axis=0)

    sink_col = None
    if has_sink:
        kv = pl.program_id(0)
        sink_col = jnp.concatenate(
            [jnp.full((tq, 1), sink_ref[kv * g + h] * LOG2E, F32) for h in range(g)], axis=0)

    def with_ones(vals):
        return jnp.concatenate([vals, jnp.ones(vals.shape, BF16)], axis=1)

    s = lax.dot_general(qs, kc_ref[...], _NT, preferred_element_type=F32)
    m = jnp.max(s, axis=-1, keepdims=True)
    if has_sink:
        m = jnp.maximum(m, sink_col)
    pv = jnp.dot(jnp.exp2(s - m).astype(BF16), with_ones(vc_ref[...]), preferred_element_type=F32)
    acc, l_rep = pv[:, :LANES], pv[:, LANES:]

    if n_chunks > 0:
        m_scr[...] = jnp.broadcast_to(m, (g * tq, LANES))
        l_scr[...] = l_rep
        acc_scr[...] = acc

        def scores(c):
            off = pl.multiple_of(c * tk, tk)
            return lax.dot_general(qs, k_ref[pl.ds(off, tk), :], _NT, preferred_element_type=F32)

        def update(s_scr, c):
            off = pl.multiple_of(c * tk, tk)
            vaug = with_ones(v_ref[pl.ds(off, tk), :])
            half = g * tq // 2
            for rows_ in (slice(0, half), slice(half, 2 * half)):
                sb = s_scr[rows_, :]
                m_prev = m_scr[rows_, :]
                m_new = jnp.maximum(m_prev, jnp.max(sb, axis=-1, keepdims=True))
                alpha = jnp.exp2(m_prev - m_new)
                m_rep = jnp.concatenate([m_new] * (tk // LANES), axis=1)
                pv = jnp.dot(jnp.exp2(sb - m_rep).astype(BF16), vaug, preferred_element_type=F32)
                l_scr[rows_, :] = alpha * l_scr[rows_, :] + pv[:, LANES:]
                acc_scr[rows_, :] = alpha * acc_scr[rows_, :] + pv[:, :LANES]
                m_scr[rows_, :] = m_new

        sa_scr[...] = scores(0)

        def body(j, carry):
            c0 = 2 * j
            sb_scr[...] = scores(c0 + 1)
            update(sa_scr, c0)
            sa_scr[...] = scores(jnp.minimum(c0 + 2, n_chunks - 1))
            update(sb_scr, c0 + 1)
            return carry

        lax.fori_loop(0, n_chunks // 2, body, 0)
        m, l_rep, acc = m_scr[:, 0:1], l_scr[...], acc_scr[...]

    l = l_rep[:, 0:1]
    if has_sink:
        l = l + jnp.exp2(sink_col - m)
    out = acc / l
    if mode == "diff":
        lv = lam_ref[...]
        lam = (jnp.exp(jnp.sum(lv[0:1] * lv[1:2], axis=-1, keepdims=True))
               - jnp.exp(jnp.sum(lv[2:3] * lv[3:4], axis=-1, keepdims=True)) + lam_init)
        y = out[:tq] - lam * out[tq:]
        o_ref[...] = (_rms(y, subln_ref[...]) * (1.0 - lam_init)).astype(o_ref.dtype)
    else:
        for h in range(g):
            o_ref[:, h * LANES:(h + 1) * LANES] = out[h * tq:(h + 1) * tq].astype(o_ref.dtype)


def flash(q, k, v, kc, vc, *, mode, tq, tk=1024, sink=None, lam=None, subln=None, lam_init=0.0):
    sq = q.shape[0]
    n_ctx = kc.shape[0]
    if k is None:
        k, v, n_chunks, t_main = kc, vc, 0, n_ctx
    else:
        t_main = k.shape[0]
        tk = min(tk, t_main // 2)
        n_chunks = t_main // tk
        assert n_chunks % 2 == 0 and n_chunks * tk == t_main
    tq = min(tq, sq)
    if mode == "diff":
        n_outer, g, qw = DA_HEADS, 2, LANES
    else:
        n_outer, g = k.shape[1] // LANES, q.shape[1] // k.shape[1]
        qw = g * LANES
    rows = g * tq
    kv_spec = pl.BlockSpec((t_main, LANES), lambda h, i: (0, h))
    ctx_spec = pl.BlockSpec((n_ctx, LANES), lambda h, i: (0, h))
    in_specs = [pl.BlockSpec((tq, qw), lambda h, i: (i, h)), kv_spec, kv_spec, ctx_spec, ctx_spec]
    args = [q, k, v, kc, vc]
    if mode == "diff":
        in_specs += [pl.BlockSpec(lam.shape, lambda h, i: (0, 0)),
                     pl.BlockSpec((1, LANES), lambda h, i: (0, 0))]
        args += [lam, _row(subln)]
    elif sink is not None:
        in_specs.append(pl.BlockSpec(memory_space=pltpu.SMEM))
        args.append(sink)
    return pl.pallas_call(
        functools.partial(_flash_kernel, mode=mode, g=g, tq=tq, tk=tk, n_chunks=n_chunks,
                          has_sink=sink is not None, lam_init=lam_init),
        grid=(n_outer, sq // tq), in_specs=in_specs,
        out_specs=pl.BlockSpec((tq, qw), lambda h, i: (i, h)),
        out_shape=jax.ShapeDtypeStruct((sq, n_outer * qw), BF16),
        scratch_shapes=[pltpu.VMEM((rows, LANES), F32), pltpu.VMEM((rows, LANES), F32),
                        pltpu.VMEM((rows, LANES), F32),
                        pltpu.VMEM((rows, tk), F32), pltpu.VMEM((rows, tk), F32)],
        compiler_params=_params("parallel", "arbitrary"),
    )(*args)


def _swa_kernel(q_ref, kp_ref, kn_ref, kx_ref, vp_ref, vn_ref, vx_ref, kc_ref, vc_ref, sink_ref,
                o_ref, *, g, seq, per_step):
    kv = pl.program_id(0)
    step = pl.program_id(1)
    kwin = jnp.concatenate([kp_ref[...], kn_ref[...], kx_ref[...]], axis=0)
    vwin = jnp.concatenate([vp_ref[...], vn_ref[...], vx_ref[...]], axis=0)
    rows = g * BLOCK
    sink_col = jnp.concatenate(
        [jnp.full((BLOCK, 1), sink_ref[kv * g + h] * LOG2E, F32) for h in range(g)], axis=0)
    row_in_block = lax.broadcasted_iota(jnp.int32, (rows, 3 * BLOCK), 0) & (BLOCK - 1)
    col = lax.broadcasted_iota(jnp.int32, (rows, 3 * BLOCK), 1)
    in_band = jnp.abs(row_in_block + BLOCK - col) <= WINDOW
    blocks = range(per_step)
    qs = [jnp.concatenate([q_ref[b * BLOCK:(b + 1) * BLOCK, h * LANES:(h + 1) * LANES]
                           for h in range(g)], axis=0) for b in blocks]
    s_win = [lax.dot_general(qs[b], kwin[b * BLOCK:(b + 3) * BLOCK], _NT,
                             preferred_element_type=F32) for b in blocks]
    s_ctx = [lax.dot_general(qs[b], kc_ref[...], _NT, preferred_element_type=F32) for b in blocks]
    for b in blocks:
        kpos = (step * per_step + b - 1) * BLOCK + col
        s_win[b] = jnp.where(in_band & (kpos >= 0) & (kpos < seq), s_win[b], NEG_BIG)
    m = [jnp.maximum(jnp.maximum(jnp.max(s_win[b], axis=-1, keepdims=True),
                                 jnp.max(s_ctx[b], axis=-1, keepdims=True)), sink_col)
         for b in blocks]
    p_win = [jnp.exp2(s_win[b] - m[b]) for b in blocks]
    p_ctx = [jnp.exp2(s_ctx[b] - m[b]) for b in blocks]
    l = [jnp.sum(p_win[b], axis=-1, keepdims=True) + jnp.sum(p_ctx[b], axis=-1, keepdims=True)
         + jnp.exp2(sink_col - m[b]) for b in blocks]
    out = [(jnp.dot(p_win[b].astype(BF16), vwin[b * BLOCK:(b + 3) * BLOCK],
                    preferred_element_type=F32)
            + jnp.dot(p_ctx[b].astype(BF16), vc_ref[...], preferred_element_type=F32)) / l[b]
           for b in blocks]
    for b in blocks:
        for h in range(g):
            o_ref[b * BLOCK:(b + 1) * BLOCK, h * LANES:(h + 1) * LANES] = (
                out[b][h * BLOCK:(h + 1) * BLOCK].astype(o_ref.dtype))


def swa(q, k, v, kc, vc, sink, *, per_step=4):
    seq = q.shape[0]
    nb = seq // BLOCK
    per_step = math.gcd(per_step, nb)
    n_kv = k.shape[1] // LANES
    g = q.shape[1] // k.shape[1]
    n_ctx = kc.shape[0]
    prev = pl.BlockSpec((BLOCK, LANES), lambda h, i: (jnp.maximum(i * per_step - 1, 0), h))
    cur = pl.BlockSpec((per_step * BLOCK, LANES), lambda h, i: (i, h))
    nxt = pl.BlockSpec((BLOCK, LANES), lambda h, i: (jnp.minimum((i + 1) * per_step, nb - 1), h))
    ctx_spec = pl.BlockSpec((n_ctx, LANES), lambda h, i: (0, h))
    qspec = pl.BlockSpec((per_step * BLOCK, g * LANES), lambda h, i: (i, h))
    return pl.pallas_call(
        functools.partial(_swa_kernel, g=g, seq=seq, per_step=per_step), grid=(n_kv, nb // per_step),
        in_specs=[qspec, prev, cur, nxt, prev, cur, nxt, ctx_spec, ctx_spec,
                  pl.BlockSpec(memory_space=pltpu.SMEM)],
        out_specs=qspec,
        out_shape=jax.ShapeDtypeStruct(q.shape, BF16),
        compiler_params=_params("parallel", "parallel"),
    )(q, k, k, k, v, v, v, kc, vc, sink)


def _head_ones():
    r = lax.broadcasted_iota(jnp.int32, (LANES, LANES), 0) >> _RW_N_BITS
    c = lax.broadcasted_iota(jnp.int32, (LANES, LANES), 1) >> _RW_N_BITS
    ones = (r == c).astype(BF16)
    return jnp.concatenate([ones, ones, ones], axis=0)


def _head_sum(x, ones3):
    hi = x.astype(BF16)
    rest = x - hi.astype(F32)
    mid = rest.astype(BF16)
    lo = (rest - mid.astype(F32)).astype(BF16)
    parts = []
    for j in range(x.shape[1] // LANES):
        sl = slice(j * LANES, (j + 1) * LANES)
        parts.append(jnp.dot(jnp.concatenate([hi[:, sl], mid[:, sl], lo[:, sl]], axis=1), ones3,
                             preferred_element_type=F32))
    return jnp.concatenate(parts, axis=1)


def _shifted(z, prev_row, next_row):
    tm = z.shape[0]
    rid = lax.broadcasted_iota(jnp.int32, (SUBLANES, z.shape[1]), 0)
    zm, zp = pltpu.roll(z, 1, 0), pltpu.roll(z, tm - 1, 0)
    zm = jnp.concatenate([jnp.where(rid == 0, prev_row, zm[:SUBLANES]), zm[SUBLANES:]], axis=0)
    zp = jnp.concatenate([zp[:tm - SUBLANES],
                          jnp.where(rid == SUBLANES - 1, next_row, zp[tm - SUBLANES:])], axis=0)
    return zm, zp


def _halo_rows(zp_ref, zn_ref, n_blocks):
    i = pl.program_id(0)
    prev_row = jnp.where(i > 0, zp_ref[SUBLANES - 1:SUBLANES, :], 0.0)
    next_row = jnp.where(i < n_blocks - 1, zn_ref[0:1, :], 0.0)
    return prev_row, next_row


def _rw_prep_kernel(z_ref, zp_ref, zn_ref, sw_ref, kk_ref, w0_ref, wup_ref, a0_ref, aup_ref,
                    gup_ref, r_o, k_o, v_o, kk_o, lwf_o, lwb_o, iclf_o, iclb_o, gate_o, *, n_blocks):
    z = z_ref[...]
    prev_row, next_row = _halo_rows(zp_ref, zn_ref, n_blocks)
    zm, zp = _shifted(z, prev_row, next_row)
    t = zm * sw_ref[0:1, :] + z * sw_ref[1:2, :] + zp * sw_ref[2:3, :]
    r_o[...] = t[:, 0:RW_W]
    k = t[:, RW_W:2 * RW_W]
    k_o[...] = k
    v_o[...] = t[:, 2 * RW_W:3 * RW_W]
    kk = k * kk_ref[...]
    ss = _head_sum(kk * kk, _head_ones())
    kk_o[...] = kk * lax.rsqrt(jnp.maximum(ss, 1e-24))
    base = 3 * RW_W
    for d, (lw_o, icl_o) in enumerate(((lwf_o, iclf_o), (lwb_o, iclb_o))):
        xw = t[:, base + d * RW_DECAY_R: base + (d + 1) * RW_DECAY_R]
        pre = w0_ref[d:d + 1, :] + jnp.dot(jnp.tanh(xw).astype(BF16), wup_ref[d],
                                           preferred_element_type=F32)
        lw_o[...] = -RW_DECAY_SCALE * jax.nn.sigmoid(pre)
        a_base = base + 2 * RW_DECAY_R
        xa = t[:, a_base + d * RW_A_R: a_base + (d + 1) * RW_A_R]
        icl_o[...] = jax.nn.sigmoid(a0_ref[d:d + 1, :] + jnp.dot(
            xa.astype(BF16), aup_ref[d], preferred_element_type=F32))
    xg = t[:, base + 2 * RW_DECAY_R + 2 * RW_A_R:]
    gate_o[...] = jnp.dot(jax.nn.sigmoid(xg).astype(BF16), gup_ref[...],
                          preferred_element_type=F32)


def rw_prep(zr, shift_w, k_k, w0, w_up, a0, a_up, g_up, *, tm=128):
    m, w = zr.shape
    tm = min(tm, m)
    nb = m // tm
    per8 = tm // SUBLANES
    last8 = m // SUBLANES - 1
    full = lambda a: pl.BlockSpec(a.shape, lambda i: (0,) * a.ndim)
    out_blk = pl.BlockSpec((tm, RW_W), lambda i: (i, 0))
    args = [zr, zr, zr, shift_w, _row(k_k), w0, w_up, a0, a_up, g_up]
    in_specs = [pl.BlockSpec((tm, w), lambda i: (i, 0)),
                pl.BlockSpec((SUBLANES, w), lambda i: (jnp.maximum(i * per8 - 1, 0), 0)),
                pl.BlockSpec((SUBLANES, w), lambda i: (jnp.minimum((i + 1) * per8, last8), 0)),
                ] + [full(a) for a in args[3:]]
    return pl.pallas_call(
        functools.partial(_rw_prep_kernel, n_blocks=nb), grid=(nb,),
        in_specs=in_specs, out_specs=(out_blk,) * 9,
        out_shape=(jax.ShapeDtypeStruct((m, RW_W), F32),) * 9,
        compiler_params=_params("parallel"),
    )(*args)


_MM = (((1,), (0,)), ((), ()))


def _dot1(a, b, dims=_MM):
    return lax.dot_general(a.astype(BF16), b.astype(BF16), dims, preferred_element_type=F32)


def _hilo(x):
    hi = x.astype(BF16)
    return hi, (x - hi.astype(F32)).astype(BF16)


def _dot2(a, b):
    ah = a.astype(BF16)
    bh, bl = _hilo(b)
    return jnp.dot(jnp.concatenate([ah, ah], axis=1), jnp.concatenate([bh, bl], axis=0),
                   preferred_element_type=F32)


def _scan_chunk(r, k, v, kk, lw, icl, ka, states, *, reverse):
    C = RW_CHUNK
    n_pairs = len(states)
    each = range(n_pairs)
    kd = k * (1.0 + (icl - 1.0) * ka)
    a = -kk
    b = kk * icl

    ti = lax.broadcasted_iota(jnp.int32, (C, C), 0)
    si = lax.broadcasted_iota(jnp.int32, (C, C), 1)
    before = (si >= ti) if reverse else (si <= ti)
    lw_hi = lw.astype(BF16)
    lw_rest = lw - lw_hi.astype(F32)
    lw_mid = lw_rest.astype(BF16)
    lw_lo = (lw_rest - lw_mid.astype(F32)).astype(BF16)
    tri = before.astype(BF16)
    cl = jnp.dot(jnp.concatenate([tri, tri, tri], axis=1),
                 jnp.concatenate([lw_hi, lw_mid, lw_lo], axis=0), preferred_element_type=F32)
    tot = cl[0:1, :] if reverse else cl[C - 1:C, :]
    e_neg = jnp.exp(-cl)
    rt = r * jnp.exp(cl)
    at = a * jnp.exp(cl - lw)
    bt = b * e_neg
    kt = kd * e_neg

    lane = lax.broadcasted_iota(jnp.int32, (C, LANES), 1)
    h0 = lane < RW_N
    zero = jnp.zeros((C, LANES), F32)

    def stack(x):
        return jnp.concatenate([jnp.where(h0, x, zero), jnp.where(h0, zero, x)], axis=0)

    def pair(x, p):
        return x[:, p * LANES:(p + 1) * LANES]

    la = [stack(pair(at, p)) for p in each]
    lr = [stack(pair(rt, p)) for p in each]
    vv = [stack(pair(v, p)) for p in each]
    sc = [_dot1(jnp.concatenate([la[p], lr[p]], axis=0),
                jnp.concatenate([pair(bt, p), pair(bt, p), pair(kt, p), pair(kt, p)], axis=0), _NT)
          for p in each]

    row = lax.broadcasted_iota(jnp.int32, (2 * C, 2 * C), 0)
    col = lax.broadcasted_iota(jnp.int32, (2 * C, 2 * C), 1)
    same_head = (row >> _RW_N_BITS) == (col >> _RW_N_BITS)
    strict = same_head & ((col > row) if reverse else (col < row))
    incl = same_head & ((col >= row) if reverse else (col <= row))
    x_ab = [jnp.where(strict, sc[p][0:2 * C, 0:2 * C], 0.0) for p in each]
    x_ak = [jnp.where(strict, sc[p][0:2 * C, 2 * C:4 * C], 0.0) for p in each]
    x_rb = [jnp.where(incl, sc[p][2 * C:4 * C, 0:2 * C], 0.0) for p in each]
    x_rk = [jnp.where(incl, sc[p][2 * C:4 * C, 2 * C:4 * C], 0.0) for p in each]

    def off_mask(bsz):
        bits = (2 * bsz).bit_length() - 1
        blk = (row >> bits) == (col >> bits)
        lo, hi = (row & (2 * bsz - 1)) < bsz, (col & (2 * bsz - 1)) < bsz
        return blk & ((lo & ~hi) if reverse else (~lo & hi))

    eye = (row == col).astype(F32)
    mask1 = off_mask(1)
    tinv = [eye + jnp.where(mask1, x_ab[p], 0.0) for p in each]
    bsz = 2
    while bsz < C:
        mask = off_mask(bsz)
        half = [_dot2(tinv[p], jnp.where(mask, x_ab[p], 0.0)) for p in each]
        tinv = [tinv[p] + _dot2(half[p], tinv[p]) for p in each]
        bsz *= 2

    z = [_dot1(la[p], states[p], _NT) + _dot1(x_ak[p], vv[p]) for p in each]
    u = [_dot2(tinv[p], z[p]) for p in each]
    y = [_dot1(lr[p], states[p], _NT)
         + _dot1(jnp.concatenate([x_rb[p], x_rk[p]], axis=1), jnp.concatenate([u[p], vv[p]], axis=0))
         for p in each]
    upd = [_dot1(jnp.concatenate([u[p][0:C] + u[p][C:2 * C], pair(v, p)], axis=0),
                 jnp.concatenate([pair(bt, p), pair(kt, p)], axis=0), _TN) for p in each]
    decay = jnp.exp(tot)
    new_states = [(states[p] + jnp.where(same_head, upd[p], 0.0)) * pair(decay, p) for p in each]
    ys = [y[p][0:C] + y[p][C:2 * C] for p in each]
    return ys, new_states


def _scan_kernel(r_ref, k_ref, v_ref, kk_ref, lw_ref, icl_ref, ka_ref, s0_ref, y_ref, send_ref,
                 s_scr, *, reverse, n_chunks, pairs):
    c = pl.program_id(1)

    @pl.when(c == 0)
    def _():
        s_scr[...] = s0_ref[...]

    ys, new_states = _scan_chunk(r_ref[...], k_ref[...], v_ref[...], kk_ref[...], lw_ref[...],
                                 icl_ref[...], ka_ref[...], [s_scr[p] for p in range(pairs)],
                                 reverse=reverse)
    for p in range(pairs):
        y_ref[:, p * LANES:(p + 1) * LANES] = ys[p]
        s_scr[p] = new_states[p]

    @pl.when(c == n_chunks - 1)
    def _():
        send_ref[...] = s_scr[...]


def rw_scan(r, k, v, kk, lw, icl, k_a, s0, *, reverse, pairs=8):
    t = r.shape[0]
    n_chunks = t // RW_CHUNK
    n_pairs = RW_W // LANES
    w = pairs * LANES
    if reverse:
        blk = pl.BlockSpec((RW_CHUNK, w), lambda p, c: (n_chunks - 1 - c, p))
    else:
        blk = pl.BlockSpec((RW_CHUNK, w), lambda p, c: (c, p))
    st = pl.BlockSpec((pairs, LANES, LANES), lambda p, c: (p, 0, 0))
    return pl.pallas_call(
        functools.partial(_scan_kernel, reverse=reverse, n_chunks=n_chunks, pairs=pairs),
        grid=(n_pairs // pairs, n_chunks),
        in_specs=[blk] * 6 + [pl.BlockSpec((1, w), lambda p, c: (0, p)), st],
        out_specs=(blk, st),
        out_shape=(jax.ShapeDtypeStruct((t, RW_W), F32),
                   jax.ShapeDtypeStruct((n_pairs, LANES, LANES), F32)),
        scratch_shapes=[pltpu.VMEM((pairs, LANES, LANES), F32)],
        compiler_params=_params("parallel", "arbitrary"),
    )(r, k, v, kk, lw, icl, _row(k_a), s0)


def _rw_post_kernel(yf_ref, yb_ref, r_ref, k_ref, v_ref, iclf_ref, iclb_ref, gate_ref,
                    ka_ref, rk_ref, lnw_ref, lnb_ref, o_ref):
    ones = _head_ones()
    y = yf_ref[...] + yb_ref[...]
    mu = _head_sum(y, ones) * (1.0 / RW_N)
    yc = y - mu
    var = _head_sum(yc * yc, ones) * (1.0 / RW_N)
    yn = yc * lax.rsqrt(var + RW_LN_EPS) * lnw_ref[...] + lnb_ref[...]
    k = k_ref[...]
    ka = ka_ref[...]
    rrk = r_ref[...] * rk_ref[...]
    kd_f = k * (1.0 + (iclf_ref[...] - 1.0) * ka)
    kd_b = k * (1.0 + (iclb_ref[...] - 1.0) * ka)
    bonus = (_head_sum(rrk * kd_f, ones) + _head_sum(rrk * kd_b, ones)) * v_ref[...]
    o_ref[...] = ((yn + bonus) * gate_ref[...]).astype(o_ref.dtype)


def rw_post(yf, yb, r, k, v, icl_f, icl_b, gate, k_a, r_k, ln_w, ln_b, *, tm=256):
    m = yf.shape[0]
    tm = min(tm, m)
    blk = pl.BlockSpec((tm, RW_W), lambda i: (i, 0))
    vec = pl.BlockSpec((1, RW_W), lambda i: (0, 0))
    return pl.pallas_call(
        _rw_post_kernel, grid=(m // tm,),
        in_specs=[blk] * 8 + [vec] * 4, out_specs=blk,
        out_shape=jax.ShapeDtypeStruct((m, RW_W), BF16),
        compiler_params=_params("parallel"),
    )(yf, yb, r, k, v, icl_f, icl_b, gate, _row(k_a), _row(r_k.reshape(-1)), _row(ln_w), _row(ln_b))


def _merge_kernel(ya_ref, yb_ref, yr_ref, yd_ref, zg_ref, bu_ref, gu_ref, gb_ref, o_ref):
    zg = zg_ref[...].astype(BF16)
    acc = None
    for bi, y_ref in enumerate((ya_ref, yb_ref, yr_ref, yd_ref)):
        gate = jax.nn.sigmoid(jnp.dot(zg, gu_ref[bi], preferred_element_type=F32) + gb_ref[bi])
        term = gate * jnp.dot(y_ref[...], bu_ref[bi], preferred_element_type=F32)
        acc = term if acc is None else acc + term
    o_ref[...] = acc.astype(o_ref.dtype)


def merge(ys, zg, branch_up, gate_up, gate_bias, *, tm=1024, tn=512):
    m = zg.shape[0]
    d = branch_up.shape[-1]
    tm, tn = min(tm, m), min(tn, d)
    yblk = pl.BlockSpec((tm, BRANCH_W), lambda i, j: (i, 0))
    return pl.pallas_call(
        _merge_kernel, grid=(m // tm, d // tn),
        in_specs=[yblk] * 4 + [pl.BlockSpec((tm, GATE_R), lambda i, j: (i, 0)),
                               pl.BlockSpec((N_BRANCH, BRANCH_W, tn), lambda i, j: (0, 0, j)),
                               pl.BlockSpec((N_BRANCH, GATE_R, tn), lambda i, j: (0, 0, j)),
                               pl.BlockSpec((N_BRANCH, 1, tn), lambda i, j: (0, 0, j))],
        out_specs=pl.BlockSpec((tm, tn), lambda i, j: (i, j)),
        out_shape=jax.ShapeDtypeStruct((m, d), BF16),
        compiler_params=_params("parallel", "parallel"),
    )(*ys, zg, branch_up, gate_up, gate_bias.reshape(N_BRANCH, 1, d))


def _conv_act_kernel(g_ref, gp_ref, gn_ref, x_ref, xp_ref, xn_ref, wg_ref, wx_ref, o_ref, *, n_blocks):
    def conv(z_ref, zp_ref, zn_ref, w_ref):
        z = z_ref[...]
        prev_row, next_row = _halo_rows(zp_ref, zn_ref, n_blocks)
        zm, zp = _shifted(z, prev_row, next_row)
        return zm * w_ref[0:1, :] + z * w_ref[1:2, :] + zp * w_ref[2:3, :]
    gate = conv(g_ref, gp_ref, gn_ref, wg_ref)
    val = conv(x_ref, xp_ref, xn_ref, wx_ref)
    o_ref[...] = (gate * jax.nn.sigmoid(gate) * val).astype(o_ref.dtype)


def conv_act(hid, w_conv, *, tm=1024, tn=1024):
    m, two_f = hid.shape
    f = two_f // 2
    tm, tn = min(tm, m), min(tn, f)
    nb, nj = m // tm, f // tn
    per8 = tm // SUBLANES
    last8 = m // SUBLANES - 1

    def specs(off):
        return [pl.BlockSpec((tm, tn), lambda i, j: (i, j + off)),
                pl.BlockSpec((SUBLANES, tn), lambda i, j: (jnp.maximum(i * per8 - 1, 0), j + off)),
                pl.BlockSpec((SUBLANES, tn), lambda i, j: (jnp.minimum((i + 1) * per8, last8), j + off))]
    wspec = lambda off: pl.BlockSpec((3, tn), lambda i, j: (0, j + off))
    return pl.pallas_call(
        functools.partial(_conv_act_kernel, n_blocks=nb), grid=(nb, nj),
        in_specs=specs(0) + specs(nj) + [wspec(0), wspec(nj)],
        out_specs=pl.BlockSpec((tm, tn), lambda i, j: (i, j)),
        out_shape=jax.ShapeDtypeStruct((m, f), BF16),
        compiler_params=_params("parallel", "parallel"),
    )(hid, hid, hid, hid, hid, hid, w_conv, w_conv)


def diff_mixer(za, zac, lam_vec, subln, lam_init, tabs, need_ctx):
    w = DA_HEADS * HEAD_DIM
    scale = DA_SUB ** -0.5 * LOG2E
    q, k, v = prep_qkv(za, (w, w, w), rope=tabs, quarter=DA_SUB // 4, scale=scale)
    qc, kc, vc = prep_qkv(zac, (w, w, w), scale=scale)
    kw = dict(mode="diff", lam=lam_vec, subln=subln, lam_init=lam_init)
    y = flash(q, k, v, kc, vc, tq=1024, **kw)
    yc = flash(qc, None, None, kc, vc, tq=256, **kw) if need_ctx else None
    return y, yc


def gqa_mixer(zb, zbc, q_gain, k_gain, tabs, need_ctx):
    wq, wk = GQA_HEADS * HEAD_DIM, GQA_KV * HEAD_DIM
    scale = HEAD_DIM ** -0.5 * LOG2E
    gains = (q_gain, k_gain)
    q, k, v = prep_qkv(zb, (wq, wk, wk), gains=gains, rope=tabs, quarter=HEAD_DIM // 4, scale=scale)
    qc, kc, vc = prep_qkv(zbc, (wq, wk, wk), gains=gains, scale=scale)
    y = flash(q, k, v, kc, vc, mode="gqa", tq=512)
    yc = flash(qc, None, None, kc, vc, mode="gqa", tq=128) if need_ctx else None
    return y, yc


def swa_mixer(zd, zdc, sink, tabs, need_ctx):
    wq, wk = SWA_HEADS * HEAD_DIM, SWA_KV * HEAD_DIM
    scale = HEAD_DIM ** -0.5 * LOG2E
    q, k, v = prep_qkv(zd, (wq, wk, wk), rope=tabs, quarter=HEAD_DIM // 4, scale=scale)
    qc, kc, vc = prep_qkv(zdc, (wq, wk, wk), scale=scale)
    y = swa(q, k, v, kc, vc, sink)
    yc = flash(qc, None, None, kc, vc, mode="gqa", tq=128, sink=sink) if need_ctx else None
    return y, yc


def rwkv_mixer(zr, zrc, p, need_ctx):
    prep_args = (p["shift"], p["k_k"], p["w0"], p["w_up"], p["a0"], p["a_up"], p["g_up"])
    r, k, v, kk, lwf, lwb, iclf, iclb, gate = rw_prep(zr, *prep_args)
    rc, kc, vc, kkc, lwfc, lwbc, iclfc, iclbc, gatec = rw_prep(zrc, *prep_args)
    s0 = jnp.zeros((RW_W // LANES, LANES, LANES), F32)
    ys, ycs = [], []
    for reverse, lw, icl, lwc, iclc in ((False, lwf, iclf, lwfc, iclfc), (True, lwb, iclb, lwbc, iclbc)):
        yc_d, s_ctx = rw_scan(rc, kc, vc, kkc, lwc, iclc, p["k_a"], s0, reverse=reverse)
        y_d, _ = rw_scan(r, k, v, kk, lw, icl, p["k_a"], s_ctx, reverse=reverse)
        ys.append(y_d)
        ycs.append(yc_d)
    post_args = (p["k_a"], p["r_k"], p["ln_w"], p["ln_b"])
    y = rw_post(ys[0], ys[1], r, k, v, iclf, iclb, gate, *post_args)
    yc = rw_post(ycs[0], ycs[1], rc, kc, vc, iclfc, iclbc, gatec, *post_args) if need_ctx else None
    return y, yc


_SLAB_TILES = ((1024, 512), (1024, 512), (512, 768), (512, 768), (1024, 256))


def kernel(x, c, ctx, c_ctx, mod_down, mod_up, mod_bias, norm_mix_pre, norm_mix_post, norm_ffn_pre, norm_ffn_post, w_in, diff_lambda, diff_subln, gqa_q_norm, gqa_k_norm, rwkv_shift, rwkv_w0, rwkv_w_up, rwkv_a0, rwkv_a_up, rwkv_g_up, rwkv_k_k, rwkv_k_a, rwkv_r_k, rwkv_ln_w, rwkv_ln_b, swa_sink, branch_up, gate_up, gate_bias, w_out, ffn_up, ffn_conv, ffn_down):
    depth = w_in.shape[0]
    s, d = x.shape[1], x.shape[2]
    n_rows = s // GRID_W
    tabs_h = rope_tables(n_rows, HEAD_DIM)
    tabs_s = rope_tables(n_rows, DA_SUB)
    h, hc = x.reshape(s, d), ctx.reshape(ctx.shape[1], d)
    cvec = jnp.zeros((16, d), F32).at[0].set(c[0]).at[1].set(c_ctx)
    mod_rows = []
    for l in range(depth):
        md = matmul(cvec, mod_down[l].astype(BF16), pre_silu=True, tm=16).astype(BF16)
        mods = matmul(md, mod_up[l].astype(BF16), bias=_row(mod_bias[l]), tm=16, tn=2048)
        mod_rows.append((mods[0].reshape(N_MOD, d), mods[1].reshape(N_MOD, d)))
    u = rms_mod(h, norm_mix_pre[0], mod_rows[0][0][0], mod_rows[0][0][1])
    uc = rms_mod(hc, norm_mix_pre[0], mod_rows[0][1][0], mod_rows[0][1][1])
    for l in range(depth):
        need_ctx = l < depth - 1
        lam_init = 0.8 - 0.6 * math.exp(-0.3 * l)
        m, mc = mod_rows[l]
        rw_p = dict(shift=rwkv_shift[l], k_k=rwkv_k_k[l], w0=rwkv_w0[l], w_up=rwkv_w_up[l].astype(BF16),
                    a0=rwkv_a0[l], a_up=rwkv_a_up[l].astype(BF16), g_up=rwkv_g_up[l].astype(BF16),
                    k_a=rwkv_k_a[l], r_k=rwkv_r_k[l], ln_w=rwkv_ln_w[l], ln_b=rwkv_ln_b[l])
        bu, gu = branch_up[l].astype(BF16), gate_up[l].astype(BF16)
        fd = ffn_down[l].astype(BF16)

        def in_proj(a):
            outs, col0 = [], 0
            for width, (tm, tn) in zip(IN_SIZES, _SLAB_TILES):
                outs.append(matmul_w(a, w_in, l, col0, width, tm=tm, tn=tn))
                col0 += width
            return outs

        za, zb, zr, zd, zg = in_proj(u)
        zac, zbc, zrc, zdc, zgc = in_proj(uc)
        ya, yac = diff_mixer(za, zac, diff_lambda[l], diff_subln[l], lam_init, tabs_s, need_ctx)
        yb, ybc = gqa_mixer(zb, zbc, gqa_q_norm[l], gqa_k_norm[l], tabs_h, need_ctx)
        yr, yrc = rwkv_mixer(zr, zrc, rw_p, need_ctx)
        yd, ydc = swa_mixer(zd, zdc, swa_sink[l], tabs_h, need_ctx)

        def sublayers(hh, ys, zgate, mm, mm_next):
            acc = merge(ys, zgate, bu, gu, gate_bias[l])
            mix = matmul_w(acc, w_out, l, 0, d)
            hh, u2 = resid_norm(hh, mix, norm_mix_post[l], mm[2], (norm_ffn_pre[l], mm[3], mm[4]))
            hid = matmul_w(u2, ffn_up, l, 0, ffn_up.shape[2], tm=512, tn=1024)
            act = conv_act(hid, ffn_conv[l])
            f = matmul(act, fd, tm=512)
            if mm_next is None:
                return resid_norm(hh, f, norm_ffn_post[l], mm[5]), None
            return resid_norm(hh, f, norm_ffn_post[l], mm[5],
                              (norm_mix_pre[l + 1], mm_next[0], mm_next[1]))

        h, u = sublayers(h, (ya, yb, yr, yd), zg, m, mod_rows[l + 1][0] if need_ctx else None)
        if need_ctx:
            hc, uc = sublayers(hc, (yac, ybc, yrc, ydc), zgc, mc, mod_rows[l + 1][1])
    return h.reshape(1, s, d)
```
